```python
import math
import jax, jax.numpy as jnp
from jax import lax
import numpy as np

D_MODEL = 1024
BATCH = 8
SEQ = 2048
DEPTH = 1

W_M = D_MODEL // 2
H_M = 4
DH_M = W_M // H_M
CONV_QK = 4
CHUNK = 64
W_D = D_MODEL - W_M
H_D = 4
DV_D = W_D // H_D
DQK_D = DV_D // 2
W_MIX = W_M + W_D
D_FF = ((8 * D_MODEL) // 3 + 127) // 128 * 128
CONV_FFN = 3
ROPE_THETA = 10000.0
Q_BLOCK = 128
EPS = 1e-6
IN_SIZES = (W_M, W_M, W_M, W_M, H_M, H_M, 2 * W_D // 2 * 1, 2 * W_D // 2 * 1, W_D)
IN_COLS = sum(IN_SIZES)

kernel_name = 'hybrid_mlstm_diffattn_convffn_adaln'


def rmsnorm(x, g):
    xf = x.astype(jnp.float32)
    y = xf * lax.rsqrt(jnp.mean(xf * xf, axis=-1, keepdims=True) + EPS)
    return (y * g.astype(jnp.float32)).astype(x.dtype)


def modulate(h, shift, scale):
    return h * (1 + scale[:, None, :]) + shift[:, None, :]


def causal_dwconv(x, w, b):
    K = w.shape[0]
    y = lax.conv_general_dilated(x, w[:, None, :].astype(x.dtype), window_strides=(1,),
                                 padding=[(K - 1, 0)], dimension_numbers=('NWC', 'WIO', 'NWC'),
                                 feature_group_count=x.shape[-1])
    return y + b.astype(x.dtype)


def rope(x, cos, sin):
    half = x.shape[-1] // 2
    xf = x.astype(jnp.float32)
    x1, x2 = xf[..., :half], xf[..., half:]
    return jnp.concatenate([x1 * cos - x2 * sin, x2 * cos + x1 * sin], axis=-1).astype(x.dtype)


def mlstm_chunkwise(q, k, v, i_pre, f_pre):
    B, S, H, D = q.shape
    nc = S // CHUNK
    def to_chunks(t):
        return t.astype(jnp.float32).transpose(0, 2, 1, 3).reshape(B, H, nc, CHUNK, D).transpose(2, 0, 1, 3, 4)
    def gate_chunks(g):
        return g.astype(jnp.float32).transpose(0, 2, 1).reshape(B, H, nc, CHUNK).transpose(2, 0, 1, 3)
    qc, kc, vc = to_chunks(q), to_chunks(k * (D ** -0.5)), to_chunks(v)
    ic = gate_chunks(i_pre)
    fc = gate_chunks(jax.nn.log_sigmoid(f_pre.astype(jnp.float32)))
    tri = jnp.tril(jnp.ones((CHUNK, CHUNK), dtype=bool))

    def step(carry, inp):
        C, n, m = carry
        qb, kb, vb, ig, lf = inp
        b = jnp.cumsum(lf, axis=-1)
        dmat = jnp.where(tri, b[..., :, None] - b[..., None, :] + ig[..., None, :], -jnp.inf)
        inter = b + m[..., None]
        m_t = jnp.maximum(inter, jnp.max(dmat, axis=-1))
        w_intra = jnp.exp(dmat - m_t[..., None]) * jnp.einsum('bhtd,bhsd->bhts', qb, kb)
        w_inter = jnp.exp(inter - m_t)
        num = (w_inter[..., None] * jnp.einsum('bhvk,bhtk->bhtv', C, qb)
               + jnp.einsum('bhts,bhsv->bhtv', w_intra, vb))
        den = w_inter * jnp.einsum('bhk,bhtk->bht', n, qb) + jnp.sum(w_intra, axis=-1)
        h = num / jnp.maximum(jnp.abs(den), jnp.exp(-m_t))[..., None]
        b_last = b[..., -1]
        log_s = b_last[..., None] - b + ig
        m_new = jnp.maximum(b_last + m, jnp.max(log_s, axis=-1))
        w_s = jnp.exp(log_s - m_new[..., None])
        decay = jnp.exp(b_last + m - m_new)
        C = decay[..., None, None] * C + jnp.einsum('bhs,bhsv,bhsk->bhvk', w_s, vb, kb)
        n = decay[..., None] * n + jnp.einsum('bhs,bhsk->bhk', w_s, kb)
        return (C, n, m_new), h

    init = (jnp.zeros((B, H, D, D), jnp.float32), jnp.zeros((B, H, D), jnp.float32),
            jnp.zeros((B, H), jnp.float32))
    _, hs = lax.scan(step, init, (qc, kc, vc, ic, fc))
    return hs.transpose(1, 0, 3, 2, 4).reshape(B, S, H, D).astype(q.dtype)


def diff_attention(q, k, v, lam):
    B, S, H, _, d = q.shape
    dv = v.shape[-1]
    nq = S // Q_BLOCK
    kt = k.transpose(0, 2, 3, 1, 4)
    vt = v.transpose(0, 2, 1, 3)
    qb = q.transpose(0, 2, 3, 1, 4).reshape(B, H, 2, nq, Q_BLOCK, d).transpose(3, 0, 1, 2, 4, 5)
    key_pos = jnp.arange(S)
    scale = d ** -0.5

    def block(args):
        qi, idx = args
        s = jnp.einsum('bhcqd,bhckd->bhcqk', qi, kt).astype(jnp.float32) * scale
        q_pos = idx * Q_BLOCK + jnp.arange(Q_BLOCK)
        mask = key_pos[None, :] <= q_pos[:, None]
        p = jax.nn.softmax(jnp.where(mask, s, -jnp.inf), axis=-1)
        a = p[:, :, 0] - lam * p[:, :, 1]
        return jnp.einsum('bhqk,bhkv->bhqv', a.astype(vt.dtype), vt)

    out = lax.map(block, (qb, jnp.arange(nq)))
    return out.transpose(1, 0, 3, 2, 4).reshape(B, S, H, dv)


def setup_inputs(seed: int = 0) -> dict:
    key = jax.random.key(seed)
    ks = jax.random.split(key, 24)
    nrm = lambda k, shape, s: jax.random.normal(k, shape, jnp.float32) * s
    x = nrm(ks[0], (BATCH, SEQ, D_MODEL), 1.0)
    c = nrm(ks[1], (BATCH, D_MODEL), 1.0)
    start = jax.random.randint(ks[2], (BATCH, 1), 0, 4096, dtype=jnp.int32)
    positions = (start + jnp.arange(SEQ, dtype=jnp.int32)[None, :]).astype(jnp.int32)
    b_i = nrm(ks[8], (DEPTH, H_M), 0.1)
    b_f = jnp.linspace(3.0, 6.0, H_M, dtype=jnp.float32)[None, :] + nrm(ks[9], (DEPTH, H_M), 0.1)
    return {
        'x': x,
        'c': c,
        'positions': positions,
        'w_ada': nrm(ks[3], (DEPTH, D_MODEL, 6 * D_MODEL), D_MODEL ** -0.5),
        'b_ada': nrm(ks[4], (DEPTH, 6 * D_MODEL), 0.02),
        'g_mix': 1.0 + nrm(ks[5], (DEPTH, D_MODEL), 0.05),
        'w_in': nrm(ks[6], (DEPTH, D_MODEL, IN_COLS), D_MODEL ** -0.5),
        'conv_qk_w': nrm(ks[7], (DEPTH, CONV_QK, 2 * W_M), CONV_QK ** -0.5),
        'conv_qk_b': nrm(ks[10], (DEPTH, 2 * W_M), 0.02),
        'b_if': jnp.concatenate([b_i, b_f], axis=-1),
        'g_mlstm': 1.0 + nrm(ks[11], (DEPTH, W_M), 0.05),
        'lam_q1': nrm(ks[12], (DEPTH, DQK_D), 0.1),
        'lam_k1': nrm(ks[13], (DEPTH, DQK_D), 0.1),
        'lam_q2': nrm(ks[14], (DEPTH, DQK_D), 0.1),
        'lam_k2': nrm(ks[15], (DEPTH, DQK_D), 0.1),
        'g_diff': 1.0 + nrm(ks[16], (DEPTH, W_D), 0.05),
        'w_out': nrm(ks[17], (DEPTH, W_MIX, D_MODEL), W_MIX ** -0.5),
        'g_ffn': 1.0 + nrm(ks[18], (DEPTH, D_MODEL), 0.05),
        'w_up': nrm(ks[19], (DEPTH, D_MODEL, 2 * D_FF), D_MODEL ** -0.5),
        'conv_ffn_w': nrm(ks[20], (DEPTH, CONV_FFN, 2 * D_FF), CONV_FFN ** -0.5),
        'conv_ffn_b': nrm(ks[21], (DEPTH, 2 * D_FF), 0.02),
        'w_down': nrm(ks[22], (DEPTH, D_FF, D_MODEL), D_FF ** -0.5),
        'g_final': 1.0 + nrm(ks[23], (D_MODEL,), 0.05),
    }


def reference(x, c, positions, w_ada, b_ada, g_mix, w_in, conv_qk_w, conv_qk_b, b_if, g_mlstm,
              lam_q1, lam_k1, lam_q2, lam_k2, g_diff, w_out, g_ffn, w_up, conv_ffn_w, conv_ffn_b,
              w_down, g_final):
    B, S, _ = x.shape
    half = DQK_D // 2
    inv_freq = ROPE_THETA ** (-jnp.arange(half, dtype=jnp.float32) / half)
    ang = positions.astype(jnp.float32)[..., None] * inv_freq
    cos = jnp.cos(ang)[:, :, None, None, :]
    sin = jnp.sin(ang)[:, :, None, None, :]
    c_act = jax.nn.silu(c)
    splits = list(np.cumsum(IN_SIZES)[:-1])

    for l in range(DEPTH):
        mod = c_act @ w_ada[l] + b_ada[l]
        sh_a, sc_a, gt_a, sh_f, sc_f, gt_f = jnp.split(mod, 6, axis=-1)

        h = modulate(rmsnorm(x, g_mix[l]), sh_a, sc_a)
        proj = h @ w_in[l]
        q_m, k_m, v_m, o_m, i_m, f_m, q_d, k_d, v_d = jnp.split(proj, splits, axis=-1)

        qk = jax.nn.silu(causal_dwconv(jnp.concatenate([q_m, k_m], axis=-1), conv_qk_w[l], conv_qk_b[l]))
        q_m, k_m = jnp.split(qk, 2, axis=-1)
        i_m = i_m + b_if[l, :H_M]
        f_m = f_m + b_if[l, H_M:]
        hm = mlstm_chunkwise(q_m.reshape(B, S, H_M, DH_M), k_m.reshape(B, S, H_M, DH_M),
                             v_m.reshape(B, S, H_M, DH_M), i_m, f_m)
        hm = rmsnorm(hm, g_mlstm[l].reshape(H_M, DH_M)) * jax.nn.sigmoid(o_m).reshape(B, S, H_M, DH_M)

        lam_init = 0.8 - 0.6 * math.exp(-0.3 * l)
        lam = (jnp.exp(jnp.sum(lam_q1[l].astype(jnp.float32) * lam_k1[l].astype(jnp.float32)))
               - jnp.exp(jnp.sum(lam_q2[l].astype(jnp.float32) * lam_k2[l].astype(jnp.float32))) + lam_init)
        qd = rope(q_d.reshape(B, S, H_D, 2, DQK_D), cos, sin)
        kd = rope(k_d.reshape(B, S, H_D, 2, DQK_D), cos, sin)
        hd = diff_attention(qd, kd, v_d.reshape(B, S, H_D, DV_D), lam)
        hd = rmsnorm(hd, g_diff[l].reshape(H_D, DV_D)) * (1.0 - lam_init)

        mix = jnp.concatenate([hm.reshape(B, S, W_M), hd.reshape(B, S, W_D)], axis=-1)
        x = x + gt_a[:, None, :] * (mix @ w_out[l])

        h = modulate(rmsnorm(x, g_ffn[l]), sh_f, sc_f)
        u = causal_dwconv(h @ w_up[l], conv_ffn_w[l], conv_ffn_b[l])
        a, g = jnp.split(u, 2, axis=-1)
        x = x + gt_f[:, None, :] * ((jax.nn.silu(g) * a) @ w_down[l])

    return rmsnorm(x, g_final)
```

```python
import functools
import math

import jax
import jax.numpy as jnp
from jax import lax
from jax.experimental import pallas as pl
from jax.experimental.pallas import tpu as pltpu

F32 = jnp.float32
BF16 = jnp.bfloat16

D_MODEL = 1024
DEPTH = 1
W_M = D_MODEL // 2
H_M = 4
DH_M = W_M // H_M
CONV_QK = 4
W_D = D_MODEL - W_M
H_D = 4
DV_D = W_D // H_D
DQK_D = DV_D // 2
D_FF = ((8 * D_MODEL) // 3 + 127) // 128 * 128
CONV_FFN = 3
ROPE_THETA = 10000.0
EPS = 1e-6

LANES = 128
GATE_COLS = LANES
MM_COLS = 4 * W_M
DD_COLS = 3 * W_D
IN_COLS_PAD = MM_COLS + DD_COLS + GATE_COLS

TM_IN = 512
L_M = 256
HALO = 16
TQ = 256
TM_FFN = 512
N_FF_CHUNK = 2
TC_FF = D_FF // N_FF_CHUNK
VMEM_LIMIT = 56 * 1024 * 1024


def _sigmoid(v):
    return 1.0 / (1.0 + jnp.exp(-v))


def _split3(v):
    hi = v.astype(BF16)
    r1 = v - hi.astype(F32)
    mid = r1.astype(BF16)
    lo = (r1 - mid.astype(F32)).astype(BF16)
    return hi, mid, lo


def _adaln_kernel(c_ref, w_ref, b_ref, o_ref):
    c = c_ref[...]
    ca = (c * _sigmoid(c)).astype(BF16)
    o_ref[...] = jnp.dot(ca, w_ref[...].astype(BF16), preferred_element_type=F32) + b_ref[...]


def _adaln(c, w_ada, b_ada):
    bsz, d = c.shape
    n = w_ada.shape[1]
    tn = 1024
    return pl.pallas_call(
        _adaln_kernel,
        grid=(n // tn,),
        in_specs=[
            pl.BlockSpec((bsz, d), lambda j: (0, 0)),
            pl.BlockSpec((d, tn), lambda j: (0, j)),
            pl.BlockSpec((1, tn), lambda j: (0, j)),
        ],
        out_specs=pl.BlockSpec((bsz, tn), lambda j: (0, j)),
        out_shape=jax.ShapeDtypeStruct((bsz, n), F32),
        compiler_params=pltpu.CompilerParams(
            dimension_semantics=("arbitrary",), vmem_limit_bytes=VMEM_LIMIT),
        name="adaln",
    )(c, w_ada, b_ada.reshape(1, n))


def _inproj_kernel(x_ref, sh_ref, sc_ref, g_ref, w_ref, cos_ref, sin_ref, bif_ref,
                   mm_ref, dd_ref, gate_ref):
    x = x_ref[...]
    ms = jnp.mean(x * x, axis=-1, keepdims=True)
    y = x * lax.rsqrt(ms + EPS) * g_ref[...]
    h = (y * (1.0 + sc_ref[...]) + sh_ref[...]).astype(BF16)
    proj = jnp.dot(h, w_ref[...], preferred_element_type=F32)
    mm_ref[...] = proj[:, :MM_COLS].astype(BF16)
    cos = cos_ref[...]
    sin = sin_ref[...]
    lane = lax.broadcasted_iota(jnp.int32, cos.shape, 1)
    first_half = (lane & (DQK_D // 2)) == 0
    for j in range(2 * W_D // LANES):
        xs = proj[:, MM_COLS + j * LANES: MM_COLS + (j + 1) * LANES]
        partner = jnp.where(first_half,
                            pltpu.roll(xs, LANES - DQK_D // 2, 1),
                            pltpu.roll(xs, DQK_D // 2, 1))
        dd_ref[:, j * LANES:(j + 1) * LANES] = (xs * cos + partner * sin).astype(BF16)
    dd_ref[:, 2 * W_D:] = proj[:, MM_COLS + 2 * W_D: MM_COLS + DD_COLS].astype(BF16)
    gate_ref[...] = proj[:, MM_COLS + DD_COLS:] + bif_ref[...]


def _inproj(x2, mod3, g_mix, w_in_p, cos_t, sin_t, bif_p, seq):
    t, d = x2.shape
    tiles_per_seq = seq // TM_IN
    return pl.pallas_call(
        _inproj_kernel,
        grid=(t // TM_IN,),
        in_specs=[
            pl.BlockSpec((TM_IN, d), lambda i: (i, 0)),
            pl.BlockSpec((None, 1, d), lambda i: (i // tiles_per_seq, 0, 0)),
            pl.BlockSpec((None, 1, d), lambda i: (i // tiles_per_seq, 0, 1)),
            pl.BlockSpec((1, d), lambda i: (0, 0)),
            pl.BlockSpec((d, IN_COLS_PAD), lambda i: (0, 0)),
            pl.BlockSpec((TM_IN, LANES), lambda i: (i, 0)),
            pl.BlockSpec((TM_IN, LANES), lambda i: (i, 0)),
            pl.BlockSpec((1, GATE_COLS), lambda i: (0, 0)),
        ],
        out_specs=[
            pl.BlockSpec((TM_IN, MM_COLS), lambda i: (i, 0)),
            pl.BlockSpec((TM_IN, DD_COLS), lambda i: (i, 0)),
            pl.BlockSpec((TM_IN, GATE_COLS), lambda i: (i, 0)),
        ],
        out_shape=[
            jax.ShapeDtypeStruct((t, MM_COLS), BF16),
            jax.ShapeDtypeStruct((t, DD_COLS), BF16),
            jax.ShapeDtypeStruct((t, GATE_COLS), F32),
        ],
        compiler_params=pltpu.CompilerParams(
            dimension_semantics=("arbitrary",), vmem_limit_bytes=VMEM_LIMIT),
        name="inproj",
    )(x2, mod3, mod3, g_mix, w_in_p, cos_t, sin_t, bif_p)


def _mlstm_kernel(q_ref, k_ref, v_ref, o_ref, gate_ref, cw_ref, cb_ref, g_ref,
                  out_ref, s_ref):
    n_chunks = q_ref.shape[0] // L_M
    s_ref[...] = jnp.zeros_like(s_ref)
    row = lax.broadcasted_iota(jnp.int32, (L_M, L_M), 0)
    col = lax.broadcasted_iota(jnp.int32, (L_M, L_M), 1)
    tri = row >= col
    tri_b = tri.astype(BF16)
    lane = lax.broadcasted_iota(jnp.int32, (L_M, LANES), 1)
    ones_col = (lane == 0).astype(BF16)

    def conv_silu(ref, c, start, w, b):
        cur = ref[pl.ds(start, L_M), :].astype(F32)
        hstart = pl.multiple_of(jnp.maximum(start - HALO, 0), HALO)
        halo = ref[pl.ds(hstart, HALO), :].astype(F32)
        halo = jnp.where(c > 0, halo, 0.0)
        ext = jnp.concatenate([halo, cur], axis=0)
        y = cur * w[CONV_QK - 1:CONV_QK, :] + b
        for sh in range(1, CONV_QK):
            shifted = pltpu.roll(ext, sh, 0)[HALO:, :]
            y = y + shifted * w[CONV_QK - 1 - sh:CONV_QK - sh, :]
        return y * _sigmoid(y)

    def body(c, m_prev):
        start = pl.multiple_of(c * L_M, L_M)
        qa = conv_silu(q_ref, c, start, cw_ref[:, :W_M], cb_ref[:, :W_M])
        ka = conv_silu(k_ref, c, start, cw_ref[:, W_M:], cb_ref[:, W_M:]) * (DH_M ** -0.5)
        va = v_ref[pl.ds(start, L_M), :]
        oa = o_ref[pl.ds(start, L_M), :].astype(F32)
        g = gate_ref[pl.ds(start, L_M), :]
        lf = jnp.minimum(g, 0.0) - jnp.log1p(jnp.exp(-jnp.abs(g)))
        hi, mid, lo = _split3(lf)
        b_all = (jnp.dot(tri_b, hi, preferred_element_type=F32)
                 + jnp.dot(tri_b, mid, preferred_element_type=F32)
                 + jnp.dot(tri_b, lo, preferred_element_type=F32))
        zt = jnp.where(lane < H_M, g, b_all).T
        m_out = []
        for hh in range(H_M):
            sl = slice(hh * DH_M, (hh + 1) * DH_M)
            i_col = g[:, hh:hh + 1]
            b_col = b_all[:, H_M + hh:H_M + hh + 1]
            i_row = zt[hh:hh + 1, :]
            b_row = zt[H_M + hh:H_M + hh + 1, :]
            m_p = m_prev[hh]
            dmat = jnp.where(tri, b_col - b_row + i_row, -jnp.inf)
            inter = b_col + m_p
            m_t = jnp.maximum(inter, jnp.max(dmat, axis=-1, keepdims=True))
            qh = qa[:, sl].astype(BF16)
            kh = ka[:, sl]
            s_qk = lax.dot_general(qh, kh.astype(BF16), (((1,), (1,)), ((), ())),
                                   preferred_element_type=F32)
            w_intra = (jnp.exp(dmat - m_t) * s_qk).astype(BF16)
            w_inter = jnp.exp(inter - m_t)
            v_aug = jnp.concatenate([va[:, sl], ones_col], axis=1)
            s_old = s_ref[hh]
            tot = (w_inter * jnp.dot(qh, s_old.astype(BF16), preferred_element_type=F32)
                   + jnp.dot(w_intra, v_aug, preferred_element_type=F32))
            num = tot[:, :DH_M]
            den = tot[:, DH_M:DH_M + 1]
            hval = num / jnp.maximum(jnp.abs(den), jnp.exp(-m_t))
            b_last = b_col[L_M - 1:L_M, :]
            log_s = b_last - b_col + i_col
            m_new = jnp.maximum(b_last + m_p, jnp.max(log_s, axis=0, keepdims=True))
            w_s = jnp.exp(log_s - m_new)
            decay = jnp.exp(b_last + m_p - m_new)
            kw = (kh * w_s).astype(BF16)
            s_ref[hh] = decay * s_old + lax.dot_general(
                kw, v_aug, (((0,), (0,)), ((), ())), preferred_element_type=F32)
            m_out.append(m_new)
            hn = hval * lax.rsqrt(jnp.mean(hval * hval, axis=-1, keepdims=True) + EPS)
            out_ref[pl.ds(start, L_M), sl] = (
                hn * g_ref[:, sl] * _sigmoid(oa[:, sl])).astype(BF16)
        return tuple(m_out)

    m0 = tuple(jnp.zeros((1, 1), F32) for _ in range(H_M))
    lax.fori_loop(0, n_chunks, body, m0)


def _mlstm(mm, gates, conv_w, conv_b, g_mlstm, bsz, seq):
    t = mm.shape[0]
    return pl.pallas_call(
        _mlstm_kernel,
        grid=(bsz,),
        in_specs=[
            pl.BlockSpec((seq, W_M), lambda b: (b, 0)),
            pl.BlockSpec((seq, W_M), lambda b: (b, 1)),
            pl.BlockSpec((seq, W_M), lambda b: (b, 2)),
            pl.BlockSpec((seq, W_M), lambda b: (b, 3)),
            pl.BlockSpec((seq, GATE_COLS), lambda b: (b, 0)),
            pl.BlockSpec((CONV_QK, 2 * W_M), lambda b: (0, 0)),
            pl.BlockSpec((1, 2 * W_M), lambda b: (0, 0)),
            pl.BlockSpec((1, W_M), lambda b: (0, 0)),
        ],
        out_specs=pl.BlockSpec((seq, W_M), lambda b: (b, 0)),
        out_shape=jax.ShapeDtypeStruct((t, W_M), BF16),
        scratch_shapes=[pltpu.VMEM((H_M, DH_M, 2 * DH_M), F32)],
        compiler_params=pltpu.CompilerParams(
            dimension_semantics=("arbitrary",), vmem_limit_bytes=VMEM_LIMIT),
        name="mlstm",
    )(mm, mm, mm, mm, gates, conv_w, conv_b, g_mlstm)


def _diffattn_kernel(q_ref, k_ref, v_ref, lq1_ref, lk1_ref, lq2_ref, lk2_ref, g_ref,
                     out_ref, qs_ref, m_ref, acc_ref, *, lam_init):
    qi = pl.program_id(1)
    lane = lax.broadcasted_iota(jnp.int32, (TQ, LANES), 1)
    comp0 = lane < DQK_D
    ones_col = (lane == 0).astype(BF16)
    for hh in range(H_D):
        q = q_ref[:, hh * DV_D:(hh + 1) * DV_D]
        zero = jnp.zeros_like(q)
        qs_ref[hh] = jnp.concatenate([jnp.where(comp0, q, zero), jnp.where(comp0, zero, q)], axis=0)
    m_ref[...] = jnp.full_like(m_ref, -jnp.inf)
    acc_ref[...] = jnp.zeros_like(acc_ref)
    row = lax.broadcasted_iota(jnp.int32, (2 * TQ, TQ), 0)
    col = lax.broadcasted_iota(jnp.int32, (2 * TQ, TQ), 1)
    causal = col <= (row & (TQ - 1))

    def step(kb, masked):
        start = pl.multiple_of(kb * TQ, TQ)
        for hh in range(H_D):
            sl = slice(hh * DV_D, (hh + 1) * DV_D)
            k = k_ref[pl.ds(start, TQ), sl]
            v = v_ref[pl.ds(start, TQ), sl]
            s = lax.dot_general(qs_ref[hh], k, (((1,), (1,)), ((), ())),
                                preferred_element_type=F32)
            if masked:
                s = jnp.where(causal, s, -jnp.inf)
            m_old = m_ref[hh]
            m_new = jnp.maximum(m_old, jnp.max(s, axis=-1, keepdims=True))
            alpha = jnp.exp(m_old - m_new)
            p = jnp.exp(s - m_new).astype(BF16)
            v_aug = jnp.concatenate([v, ones_col], axis=1)
            acc_ref[hh] = alpha * acc_ref[hh] + jnp.dot(p, v_aug, preferred_element_type=F32)
            m_ref[hh] = m_new

    def loop_body(kb, carry):
        step(kb, False)
        return carry

    lax.fori_loop(0, qi, loop_body, 0)
    step(qi, True)

    lam = (jnp.exp(jnp.sum(lq1_ref[...] * lk1_ref[...], axis=-1, keepdims=True))
           - jnp.exp(jnp.sum(lq2_ref[...] * lk2_ref[...], axis=-1, keepdims=True)) + lam_init)
    for hh in range(H_D):
        sl = slice(hh * DV_D, (hh + 1) * DV_D)
        a = acc_ref[hh]
        a0 = a[:TQ]
        a1 = a[TQ:]
        o = (a0[:, :DV_D] / a0[:, DV_D:DV_D + 1]
             - lam * (a1[:, :DV_D] / a1[:, DV_D:DV_D + 1]))
        on = o * lax.rsqrt(jnp.mean(o * o, axis=-1, keepdims=True) + EPS)
        out_ref[:, sl] = (on * g_ref[:, sl] * (1.0 - lam_init)).astype(BF16)


def _diffattn(dd, lq1, lk1, lq2, lk2, g_diff, bsz, seq, lam_init):
    t = dd.shape[0]
    nq = seq // TQ
    lam_spec = pl.BlockSpec((1, DQK_D), lambda b, i: (0, 0))
    return pl.pallas_call(
        functools.partial(_diffattn_kernel, lam_init=lam_init),
        grid=(bsz, nq),
        in_specs=[
            pl.BlockSpec((TQ, W_D), lambda b, i: (b * nq + i, 0)),
            pl.BlockSpec((seq, W_D), lambda b, i: (b, 1)),
            pl.BlockSpec((seq, W_D), lambda b, i: (b, 2)),
            lam_spec, lam_spec, lam_spec, lam_spec,
            pl.BlockSpec((1, W_D), lambda b, i: (0, 0)),
        ],
        out_specs=pl.BlockSpec((TQ, W_D), lambda b, i: (b * nq + i, 0)),
        out_shape=jax.ShapeDtypeStruct((t, W_D), BF16),
        scratch_shapes=[
            pltpu.VMEM((H_D, 2 * TQ, DV_D), BF16),
            pltpu.VMEM((H_D, 2 * TQ, 1), F32),
            pltpu.VMEM((H_D, 2 * TQ, 2 * DV_D), F32),
        ],
        compiler_params=pltpu.CompilerParams(
            dimension_semantics=("arbitrary", "arbitrary"), vmem_limit_bytes=VMEM_LIMIT),
        name="diffattn",
    )(dd, dd, dd, lq1, lk1, lq2, lk2, g_diff)


def _ffn_kernel(x_ref, hm_ref, hd_ref, gta_ref, shf_ref, scf_ref, gtf_ref, gffn_ref, gfin_ref,
                wout_ref, wup_ref, cw_ref, cb_ref, wdown_ref, out_ref, carry_ref, *, tiles_per_seq):
    i = pl.program_id(0)
    seq_start = (i % tiles_per_seq) == 0
    mix = (jnp.dot(hm_ref[...], wout_ref[:W_M, :], preferred_element_type=F32)
           + jnp.dot(hd_ref[...], wout_ref[W_M:, :], preferred_element_type=F32))
    x1 = x_ref[...] + gta_ref[...] * mix
    y = x1 * lax.rsqrt(jnp.mean(x1 * x1, axis=-1, keepdims=True) + EPS) * gffn_ref[...]
    h2 = (y * (1.0 + scf_ref[...]) + shf_ref[...]).astype(BF16)
    tm = h2.shape[0]
    acc = jnp.zeros((tm, D_MODEL), F32)
    for j in range(N_FF_CHUNK):
        p = jnp.dot(h2, wup_ref[j], preferred_element_type=F32)
        prev = jnp.where(seq_start, 0.0, carry_ref[j])
        carry_ref[j] = p[tm - 8:, :]
        ext = jnp.concatenate([prev, p], axis=0)
        w = cw_ref[j]
        u = p * w[2:3, :] + cb_ref[j]
        for sh in range(1, CONV_FFN):
            shifted = pltpu.roll(ext, sh, 0)[8:, :]
            u = u + shifted * w[CONV_FFN - 1 - sh:CONV_FFN - sh, :]
        a = u[:, :TC_FF]
        g = u[:, TC_FF:]
        act = (g * _sigmoid(g) * a).astype(BF16)
        acc = acc + jnp.dot(act, wdown_ref[j], preferred_element_type=F32)
    x2 = x1 + gtf_ref[...] * acc
    out_ref[...] = x2 * lax.rsqrt(jnp.mean(x2 * x2, axis=-1, keepdims=True) + EPS) * gfin_ref[...]


def _ffn(x2, hm, hd, mod3, g_ffn, g_final, w_out_b, w_up_c, cw_c, cb_c, w_down_c, seq):
    t, d = x2.shape
    tiles_per_seq = seq // TM_FFN

    def mod_spec(j):
        return pl.BlockSpec((None, 1, d), lambda i: (i // tiles_per_seq, 0, j))

    def const(shape):
        return pl.BlockSpec(shape, lambda i: (0,) * len(shape))

    return pl.pallas_call(
        functools.partial(_ffn_kernel, tiles_per_seq=tiles_per_seq),
        grid=(t // TM_FFN,),
        in_specs=[
            pl.BlockSpec((TM_FFN, d), lambda i: (i, 0)),
            pl.BlockSpec((TM_FFN, W_M), lambda i: (i, 0)),
            pl.BlockSpec((TM_FFN, W_D), lambda i: (i, 0)),
            mod_spec(2), mod_spec(3), mod_spec(4), mod_spec(5),
            const((1, d)), const((1, d)),
            const((d, d)),
            const((N_FF_CHUNK, d, 2 * TC_FF)),
            const((N_FF_CHUNK, CONV_FFN, 2 * TC_FF)),
            const((N_FF_CHUNK, 1, 2 * TC_FF)),
            const((N_FF_CHUNK, TC_FF, d)),
        ],
        out_specs=pl.BlockSpec((TM_FFN, d), lambda i: (i, 0)),
        out_shape=jax.ShapeDtypeStruct((t, d), F32),
        scratch_shapes=[pltpu.VMEM((N_FF_CHUNK, 8, 2 * TC_FF), F32)],
        compiler_params=pltpu.CompilerParams(
            dimension_semantics=("arbitrary",), vmem_limit_bytes=VMEM_LIMIT),
        name="ffn",
    )(x2, hm, hd, mod3, mod3, mod3, mod3, g_ffn, g_final, w_out_b, w_up_c, cw_c, cb_c, w_down_c)


def _chunk_cols(a):
    lead = a.shape[:-1]
    a4 = a.reshape(lead + (2, N_FF_CHUNK, TC_FF))
    a4 = jnp.moveaxis(a4, -2, 0)
    return a4.reshape((N_FF_CHUNK,) + lead + (2 * TC_FF,))


def kernel(x, c, positions, w_ada, b_ada, g_mix, w_in, conv_qk_w, conv_qk_b, b_if, g_mlstm,
           lam_q1, lam_k1, lam_q2, lam_k2, g_diff, w_out, g_ffn, w_up, conv_ffn_w, conv_ffn_b,
           w_down, g_final):
    bsz, seq, d = x.shape
    t = bsz * seq
    half = DQK_D // 2
    inv_freq = ROPE_THETA ** (-jnp.arange(half, dtype=F32) / half)
    ang = positions.astype(F32)[..., None] * inv_freq
    cos = jnp.cos(ang).reshape(t, half)
    sin = jnp.sin(ang).reshape(t, half)
    cos_t = jnp.tile(cos, (1, LANES // half))
    sin_t = jnp.tile(jnp.concatenate([-sin, sin], axis=-1), (1, LANES // DQK_D))

    xcur = x.reshape(t, d)
    n_gate = 2 * H_M
    q_scale = jnp.concatenate([jnp.full((W_D,), DQK_D ** -0.5, F32), jnp.ones((2 * W_D,), F32)])
    for l in range(DEPTH):
        lam_init = 0.8 - 0.6 * math.exp(-0.3 * l)
        w_l = w_in[l]
        w_in_p = jnp.concatenate([
            w_l[:, :MM_COLS],
            w_l[:, MM_COLS + n_gate:] * q_scale,
            w_l[:, MM_COLS:MM_COLS + n_gate],
            jnp.zeros((d, GATE_COLS - n_gate), F32)], axis=1).astype(BF16)
        bif_p = jnp.concatenate([b_if[l], jnp.zeros((GATE_COLS - n_gate,), F32)]).reshape(1, GATE_COLS)
        w_up_c = _chunk_cols(w_up[l]).astype(BF16)
        cw_c = _chunk_cols(conv_ffn_w[l])
        cb_c = _chunk_cols(conv_ffn_b[l].reshape(1, -1))
        w_down_c = w_down[l].reshape(N_FF_CHUNK, TC_FF, d).astype(BF16)
        w_out_b = w_out[l].astype(BF16)

        mod3 = _adaln(c, w_ada[l], b_ada[l]).reshape(bsz, 1, 6 * d)
        mm, dd, gates = _inproj(xcur, mod3, g_mix[l].reshape(1, d), w_in_p, cos_t, sin_t, bif_p, seq)
        hm = _mlstm(mm, gates, conv_qk_w[l], conv_qk_b[l].reshape(1, -1),
                    g_mlstm[l].reshape(1, -1), bsz, seq)
        hd = _diffattn(dd, lam_q1[l].reshape(1, -1), lam_k1[l].reshape(1, -1),
                       lam_q2[l].reshape(1, -1), lam_k2[l].reshape(1, -1),
                       g_diff[l].reshape(1, -1), bsz, seq, lam_init)
        g_last = g_final if l == DEPTH - 1 else jnp.ones_like(g_final)
        xcur = _ffn(xcur, hm, hd, mod3, g_ffn[l].reshape(1, d), g_last.reshape(1, d),
                    w_out_b, w_up_c, cw_c, cb_c, w_down_c, seq)
    return xcur.reshape(bsz, seq, d)
```

```python
import functools
import math

import jax
import jax.numpy as jnp
from jax import lax
from jax.experimental import pallas as pl
from jax.experimental.pallas import tpu as pltpu

F32 = jnp.float32
BF16 = jnp.bfloat16

D_MODEL = 1024
DEPTH = 1
W_M = D_MODEL // 2
H_M = 4
DH_M = W_M // H_M
CONV_QK = 4
W_D = D_MODEL - W_M
H_D = 4
DV_D = W_D // H_D
DQK_D = DV_D // 2
D_FF = ((8 * D_MODEL) // 3 + 127) // 128 * 128
CONV_FFN = 3
ROPE_THETA = 10000.0
EPS = 1e-6

LANES = 128
GATE_COLS = LANES
MM_COLS = 4 * W_M
DD_COLS = 3 * W_D
IN_COLS_PAD = MM_COLS + DD_COLS + GATE_COLS

TM_IN = 512
L_M = 256
HALO = 16
TQ = 256
TM_FFN = 512
N_FF_CHUNK = 2
TC_FF = D_FF // N_FF_CHUNK
VMEM_LIMIT = 56 * 1024 * 1024


def _sigmoid(v):
    return 1.0 / (1.0 + jnp.exp(-v))


def _split3(v):
    hi = v.astype(BF16)
    r1 = v - hi.astype(F32)
    mid = r1.astype(BF16)
    lo = (r1 - mid.astype(F32)).astype(BF16)
    return hi, mid, lo


def _adaln_kernel(c_ref, w_ref, b_ref, o_ref):
    c = c_ref[...]
    ca = (c * _sigmoid(c)).astype(BF16)
    o_ref[...] = jnp.dot(ca, w_ref[...].astype(BF16), preferred_element_type=F32) + b_ref[...]


def _adaln(c, w_ada, b_ada):
    bsz, d = c.shape
    n = w_ada.shape[1]
    tn = 1024
    return pl.pallas_call(
        _adaln_kernel,
        grid=(n // tn,),
        in_specs=[
            pl.BlockSpec((bsz, d), lambda j: (0, 0)),
            pl.BlockSpec((d, tn), lambda j: (0, j)),
            pl.BlockSpec((1, tn), lambda j: (0, j)),
        ],
        out_specs=pl.BlockSpec((bsz, tn), lambda j: (0, j)),
        out_shape=jax.ShapeDtypeStruct((bsz, n), F32),
        compiler_params=pltpu.CompilerParams(
            dimension_semantics=("arbitrary",), vmem_limit_bytes=VMEM_LIMIT),
        name="adaln",
    )(c, w_ada, b_ada.reshape(1, n))


def _inproj_kernel(x_ref, sh_ref, sc_ref, g_ref, w_ref, cos_ref, sin_ref, bif_ref,
                   mm_ref, dd_ref, gate_ref):
    x = x_ref[...]
    ms = jnp.mean(x * x, axis=-1, keepdims=True)
    y = x * lax.rsqrt(ms + EPS) * g_ref[...]
    h = (y * (1.0 + sc_ref[...]) + sh_ref[...]).astype(BF16)
    proj = jnp.dot(h, w_ref[...], preferred_element_type=F32)
    mm_ref[...] = proj[:, :MM_COLS].astype(BF16)
    cos = cos_ref[...]
    sin = sin_ref[...]
    lane = lax.broadcasted_iota(jnp.int32, cos.shape, 1)
    first_half = (lane & (DQK_D // 2)) == 0
    for j in range(2 * W_D // LANES):
        xs = proj[:, MM_COLS + j * LANES: MM_COLS + (j + 1) * LANES]
        partner = jnp.where(first_half,
                            pltpu.roll(xs, LANES - DQK_D // 2, 1),
                            pltpu.roll(xs, DQK_D // 2, 1))
        dd_ref[:, j * LANES:(j + 1) * LANES] = (xs * cos + partner * sin).astype(BF16)
    dd_ref[:, 2 * W_D:] = proj[:, MM_COLS + 2 * W_D: MM_COLS + DD_COLS].astype(BF16)
    gate_ref[...] = proj[:, MM_COLS + DD_COLS:] + bif_ref[...]


def _inproj(x3, mod3, g_mix, w_in_p, cos_t, sin_t, bif_p):
    bsz, seq, d = x3.shape
    t = bsz * seq
    tiles_per_seq = seq // TM_IN
    return pl.pallas_call(
        _inproj_kernel,
        grid=(t // TM_IN,),
        in_specs=[
            pl.BlockSpec((None, TM_IN, d), lambda i: (i // tiles_per_seq, i % tiles_per_seq, 0)),
            pl.BlockSpec((None, 1, d), lambda i: (i // tiles_per_seq, 0, 0)),
            pl.BlockSpec((None, 1, d), lambda i: (i // tiles_per_seq, 0, 1)),
            pl.BlockSpec((1, d), lambda i: (0, 0)),
            pl.BlockSpec((d, IN_COLS_PAD), lambda i: (0, 0)),
            pl.BlockSpec((TM_IN, LANES), lambda i: (i, 0)),
            pl.BlockSpec((TM_IN, LANES), lambda i: (i, 0)),
            pl.BlockSpec((1, GATE_COLS), lambda i: (0, 0)),
        ],
        out_specs=[
            pl.BlockSpec((TM_IN, MM_COLS), lambda i: (i, 0)),
            pl.BlockSpec((TM_IN, DD_COLS), lambda i: (i, 0)),
            pl.BlockSpec((TM_IN, GATE_COLS), lambda i: (i, 0)),
        ],
        out_shape=[
            jax.ShapeDtypeStruct((t, MM_COLS), BF16),
            jax.ShapeDtypeStruct((t, DD_COLS), BF16),
            jax.ShapeDtypeStruct((t, GATE_COLS), F32),
        ],
        compiler_params=pltpu.CompilerParams(
            dimension_semantics=("arbitrary",), vmem_limit_bytes=VMEM_LIMIT),
        name="inproj",
    )(x3, mod3, mod3, g_mix, w_in_p, cos_t, sin_t, bif_p)


def _mlstm_kernel(q_ref, k_ref, v_ref, o_ref, gate_ref, cw_ref, cb_ref, g_ref,
                  out_ref, s_ref):
    n_chunks = q_ref.shape[0] // L_M
    s_ref[...] = jnp.zeros_like(s_ref)
    row = lax.broadcasted_iota(jnp.int32, (L_M, L_M), 0)
    col = lax.broadcasted_iota(jnp.int32, (L_M, L_M), 1)
    tri = row >= col
    tri_b = tri.astype(BF16)
    lane = lax.broadcasted_iota(jnp.int32, (L_M, LANES), 1)
    ones_col = (lane == 0).astype(BF16)

    def conv_silu(ref, c, start, w, b):
        cur = ref[pl.ds(start, L_M), :].astype(F32)
        hstart = pl.multiple_of(jnp.maximum(start - HALO, 0), HALO)
        halo = ref[pl.ds(hstart, HALO), :].astype(F32)
        halo = jnp.where(c > 0, halo, 0.0)
        ext = jnp.concatenate([halo, cur], axis=0)
        y = cur * w[CONV_QK - 1:CONV_QK, :] + b
        for sh in range(1, CONV_QK):
            shifted = pltpu.roll(ext, sh, 0)[HALO:, :]
            y = y + shifted * w[CONV_QK - 1 - sh:CONV_QK - sh, :]
        return y * _sigmoid(y)

    def body(c, m_prev):
        start = pl.multiple_of(c * L_M, L_M)
        qa = conv_silu(q_ref, c, start, cw_ref[:, :W_M], cb_ref[:, :W_M])
        ka = conv_silu(k_ref, c, start, cw_ref[:, W_M:], cb_ref[:, W_M:]) * (DH_M ** -0.5)
        va = v_ref[pl.ds(start, L_M), :]
        oa = o_ref[pl.ds(start, L_M), :].astype(F32)
        g = gate_ref[pl.ds(start, L_M), :]
        lf = jnp.minimum(g, 0.0) - jnp.log1p(jnp.exp(-jnp.abs(g)))
        hi, mid, lo = _split3(lf)
        b_all = (jnp.dot(tri_b, hi, preferred_element_type=F32)
                 + jnp.dot(tri_b, mid, preferred_element_type=F32)
                 + jnp.dot(tri_b, lo, preferred_element_type=F32))
        zt = jnp.where(lane < H_M, g, b_all).T
        m_out = []
        for hh in range(H_M):
            sl = slice(hh * DH_M, (hh + 1) * DH_M)
            i_col = g[:, hh:hh + 1]
            b_col = b_all[:, H_M + hh:H_M + hh + 1]
            i_row = zt[hh:hh + 1, :]
            b_row = zt[H_M + hh:H_M + hh + 1, :]
            m_p = m_prev[hh]
            dmat = jnp.where(tri, b_col - b_row + i_row, -jnp.inf)
            inter = b_col + m_p
            m_t = jnp.maximum(inter, jnp.max(dmat, axis=-1, keepdims=True))
            qh = qa[:, sl].astype(BF16)
            kh = ka[:, sl]
            s_qk = lax.dot_general(qh, kh.astype(BF16), (((1,), (1,)), ((), ())),
                                   preferred_element_type=F32)
            w_intra = (jnp.exp(dmat - m_t) * s_qk).astype(BF16)
            w_inter = jnp.exp(inter - m_t)
            v_aug = jnp.concatenate([va[:, sl], ones_col], axis=1)
            s_old = s_ref[hh]
            tot = (w_inter * jnp.dot(qh, s_old.astype(BF16), preferred_element_type=F32)
                   + jnp.dot(w_intra, v_aug, preferred_element_type=F32))
            num = tot[:, :DH_M]
            den = tot[:, DH_M:DH_M + 1]
            hval = num / jnp.maximum(jnp.abs(den), jnp.exp(-m_t))
            b_last = b_col[L_M - 1:L_M, :]
            log_s = b_last - b_col + i_col
            m_new = jnp.maximum(b_last + m_p, jnp.max(log_s, axis=0, keepdims=True))
            w_s = jnp.exp(log_s - m_new)
            decay = jnp.exp(b_last + m_p - m_new)
            kw = (kh * w_s).astype(BF16)
            s_ref[hh] = decay * s_old + lax.dot_general(
                kw, v_aug, (((0,), (0,)), ((), ())), preferred_element_type=F32)
            m_out.append(m_new)
            hn = hval * lax.rsqrt(jnp.mean(hval * hval, axis=-1, keepdims=True) + EPS)
            out_ref[pl.ds(start, L_M), sl] = (
                hn * g_ref[:, sl] * _sigmoid(oa[:, sl])).astype(BF16)
        return tuple(m_out)

    m0 = tuple(jnp.zeros((1, 1), F32) for _ in range(H_M))
    lax.fori_loop(0, n_chunks, body, m0)


def _mlstm(mm, gates, conv_w, conv_b, g_mlstm, bsz, seq):
    t = mm.shape[0]
    return pl.pallas_call(
        _mlstm_kernel,
        grid=(bsz,),
        in_specs=[
            pl.BlockSpec((seq, W_M), lambda b: (b, 0)),
            pl.BlockSpec((seq, W_M), lambda b: (b, 1)),
            pl.BlockSpec((seq, W_M), lambda b: (b, 2)),
            pl.BlockSpec((seq, W_M), lambda b: (b, 3)),
            pl.BlockSpec((seq, GATE_COLS), lambda b: (b, 0)),
            pl.BlockSpec((CONV_QK, 2 * W_M), lambda b: (0, 0)),
            pl.BlockSpec((1, 2 * W_M), lambda b: (0, 0)),
            pl.BlockSpec((1, W_M), lambda b: (0, 0)),
        ],
        out_specs=pl.BlockSpec((seq, W_M), lambda b: (b, 0)),
        out_shape=jax.ShapeDtypeStruct((t, W_M), BF16),
        scratch_shapes=[pltpu.VMEM((H_M, DH_M, 2 * DH_M), F32)],
        compiler_params=pltpu.CompilerParams(
            dimension_semantics=("arbitrary",), vmem_limit_bytes=VMEM_LIMIT),
        name="mlstm",
    )(mm, mm, mm, mm, gates, conv_w, conv_b, g_mlstm)


def _diffattn_kernel(q_ref, k_ref, v_ref, lq1_ref, lk1_ref, lq2_ref, lk2_ref, g_ref,
                     out_ref, vaug_ref, *, lam_init):
    seq = q_ref.shape[0]
    lane = lax.broadcasted_iota(jnp.int32, (TQ, LANES), 1)
    comp0 = lane < DQK_D
    lane_s = lax.broadcasted_iota(jnp.int32, (seq, LANES), 1)
    vaug_ref[:, :DV_D] = v_ref[...]
    vaug_ref[:, DV_D:] = (lane_s == 0).astype(BF16)
    row = lax.broadcasted_iota(jnp.int32, (2 * TQ, TQ), 0)
    col = lax.broadcasted_iota(jnp.int32, (2 * TQ, TQ), 1)
    causal = col <= (row & (TQ - 1))
    lam = (jnp.exp(jnp.sum(lq1_ref[...] * lk1_ref[...], axis=-1, keepdims=True))
           - jnp.exp(jnp.sum(lq2_ref[...] * lk2_ref[...], axis=-1, keepdims=True)) + lam_init)
    for qi in range(seq // TQ):
        n_keys = (qi + 1) * TQ
        q = q_ref[qi * TQ:(qi + 1) * TQ, :]
        zero = jnp.zeros_like(q)
        q2 = jnp.concatenate([jnp.where(comp0, q, zero), jnp.where(comp0, zero, q)], axis=0)
        s_diag = lax.dot_general(q2, k_ref[n_keys - TQ:n_keys, :], (((1,), (1,)), ((), ())),
                                 preferred_element_type=F32)
        s_diag = jnp.where(causal, s_diag, -jnp.inf)
        m = jnp.max(s_diag, axis=-1, keepdims=True)
        if qi > 0:
            s_past = lax.dot_general(q2, k_ref[:n_keys - TQ, :], (((1,), (1,)), ((), ())),
                                     preferred_element_type=F32)
            m = jnp.maximum(m, jnp.max(s_past, axis=-1, keepdims=True))
            p = jnp.concatenate([jnp.exp(s_past - m).astype(BF16),
                                 jnp.exp(s_diag - m).astype(BF16)], axis=1)
        else:
            p = jnp.exp(s_diag - m).astype(BF16)
        a = jnp.dot(p, vaug_ref[:n_keys, :], preferred_element_type=F32)
        a0 = a[:TQ]
        a1 = a[TQ:]
        o = (a0[:, :DV_D] / a0[:, DV_D:DV_D + 1]
             - lam * (a1[:, :DV_D] / a1[:, DV_D:DV_D + 1]))
        on = o * lax.rsqrt(jnp.mean(o * o, axis=-1, keepdims=True) + EPS)
        out_ref[qi * TQ:(qi + 1) * TQ, :] = (on * g_ref[...] * (1.0 - lam_init)).astype(BF16)


def _diffattn(dd, lq1, lk1, lq2, lk2, g_diff, bsz, seq, lam_init):
    t = dd.shape[0]
    lam_spec = pl.BlockSpec((1, DQK_D), lambda b, h: (0, 0))
    return pl.pallas_call(
        functools.partial(_diffattn_kernel, lam_init=lam_init),
        grid=(bsz, H_D),
        in_specs=[
            pl.BlockSpec((seq, DV_D), lambda b, h: (b, h)),
            pl.BlockSpec((seq, DV_D), lambda b, h: (b, H_D + h)),
            pl.BlockSpec((seq, DV_D), lambda b, h: (b, 2 * H_D + h)),
            lam_spec, lam_spec, lam_spec, lam_spec,
            pl.BlockSpec((1, DV_D), lambda b, h: (0, h)),
        ],
        out_specs=pl.BlockSpec((seq, DV_D), lambda b, h: (b, h)),
        out_shape=jax.ShapeDtypeStruct((t, W_D), BF16),
        scratch_shapes=[pltpu.VMEM((seq, 2 * DV_D), BF16)],
        compiler_params=pltpu.CompilerParams(
            dimension_semantics=("arbitrary", "arbitrary"), vmem_limit_bytes=VMEM_LIMIT),
        name="diffattn",
    )(dd, dd, dd, lq1, lk1, lq2, lk2, g_diff)


def _ffn_kernel(x_ref, hm_ref, hd_ref, gta_ref, shf_ref, scf_ref, gtf_ref, gffn_ref, gfin_ref,
                wout_ref, wup_ref, cw_ref, cb_ref, wdown_ref, out_ref, carry_ref, *, tiles_per_seq):
    i = pl.program_id(0)
    seq_start = (i % tiles_per_seq) == 0
    mix = (jnp.dot(hm_ref[...], wout_ref[:W_M, :], preferred_element_type=F32)
           + jnp.dot(hd_ref[...], wout_ref[W_M:, :], preferred_element_type=F32))
    x1 = x_ref[...] + gta_ref[...] * mix
    y = x1 * lax.rsqrt(jnp.mean(x1 * x1, axis=-1, keepdims=True) + EPS) * gffn_ref[...]
    h2 = (y * (1.0 + scf_ref[...]) + shf_ref[...]).astype(BF16)
    tm = h2.shape[0]
    acc = jnp.zeros((tm, D_MODEL), F32)
    for j in range(N_FF_CHUNK):
        p = jnp.dot(h2, wup_ref[j], preferred_element_type=F32)
        prev = jnp.where(seq_start, 0.0, carry_ref[j])
        carry_ref[j] = p[tm - 8:, :]
        ext = jnp.concatenate([prev, p], axis=0)
        w = cw_ref[j]
        u = p * w[2:3, :] + cb_ref[j]
        for sh in range(1, CONV_FFN):
            shifted = pltpu.roll(ext, sh, 0)[8:, :]
            u = u + shifted * w[CONV_FFN - 1 - sh:CONV_FFN - sh, :]
        a = u[:, :TC_FF]
        g = u[:, TC_FF:]
        act = (g * _sigmoid(g) * a).astype(BF16)
        acc = acc + jnp.dot(act, wdown_ref[j], preferred_element_type=F32)
    x2 = x1 + gtf_ref[...] * acc
    out_ref[...] = x2 * lax.rsqrt(jnp.mean(x2 * x2, axis=-1, keepdims=True) + EPS) * gfin_ref[...]


def _ffn(x3, hm, hd, mod3, g_ffn, g_final, w_out_b, w_up_c, cw_c, cb_c, w_down_c):
    bsz, seq, d = x3.shape
    t = bsz * seq
    tiles_per_seq = seq // TM_FFN

    def mod_spec(j):
        return pl.BlockSpec((None, 1, d), lambda i: (i // tiles_per_seq, 0, j))

    def const(shape):
        return pl.BlockSpec(shape, lambda i: (0,) * len(shape))

    x_spec = pl.BlockSpec((None, TM_FFN, d), lambda i: (i // tiles_per_seq, i % tiles_per_seq, 0))
    return pl.pallas_call(
        functools.partial(_ffn_kernel, tiles_per_seq=tiles_per_seq),
        grid=(t // TM_FFN,),
        in_specs=[
            x_spec,
            pl.BlockSpec((TM_FFN, W_M), lambda i: (i, 0)),
            pl.BlockSpec((TM_FFN, W_D), lambda i: (i, 0)),
            mod_spec(2), mod_spec(3), mod_spec(4), mod_spec(5),
            const((1, d)), const((1, d)),
            const((d, d)),
            const((N_FF_CHUNK, d, 2 * TC_FF)),
            const((N_FF_CHUNK, CONV_FFN, 2 * TC_FF)),
            const((N_FF_CHUNK, 1, 2 * TC_FF)),
            const((N_FF_CHUNK, TC_FF, d)),
        ],
        out_specs=x_spec,
        out_shape=jax.ShapeDtypeStruct((bsz, seq, d), F32),
        scratch_shapes=[pltpu.VMEM((N_FF_CHUNK, 8, 2 * TC_FF), F32)],
        compiler_params=pltpu.CompilerParams(
            dimension_semantics=("arbitrary",), vmem_limit_bytes=VMEM_LIMIT),
        name="ffn",
    )(x3, hm, hd, mod3, mod3, mod3, mod3, g_ffn, g_final, w_out_b, w_up_c, cw_c, cb_c, w_down_c)


def _chunk_cols(a):
    lead = a.shape[:-1]
    a4 = a.reshape(lead + (2, N_FF_CHUNK, TC_FF))
    a4 = jnp.moveaxis(a4, -2, 0)
    return a4.reshape((N_FF_CHUNK,) + lead + (2 * TC_FF,))


def kernel(x, c, positions, w_ada, b_ada, g_mix, w_in, conv_qk_w, conv_qk_b, b_if, g_mlstm,
           lam_q1, lam_k1, lam_q2, lam_k2, g_diff, w_out, g_ffn, w_up, conv_ffn_w, conv_ffn_b,
           w_down, g_final):
    bsz, seq, d = x.shape
    t = bsz * seq
    half = DQK_D // 2
    inv_freq = ROPE_THETA ** (-jnp.arange(half, dtype=F32) / half)
    ang = positions.astype(F32)[..., None] * inv_freq
    cos = jnp.cos(ang).reshape(t, half)
    sin = jnp.sin(ang).reshape(t, half)
    cos_t = jnp.tile(cos, (1, LANES // half))
    sin_t = jnp.tile(jnp.concatenate([-sin, sin], axis=-1), (1, LANES // DQK_D))

    assert w_ada.shape[0] == DEPTH == 1
    l = 0
    n_gate = 2 * H_M
    lam_init = 0.8 - 0.6 * math.exp(-0.3 * l)
    q_scale = jnp.concatenate([jnp.full((W_D,), DQK_D ** -0.5, F32), jnp.ones((2 * W_D,), F32)])
    w_l = w_in[l]
    w_in_p = jnp.concatenate([
        w_l[:, :MM_COLS],
        w_l[:, MM_COLS + n_gate:] * q_scale,
        w_l[:, MM_COLS:MM_COLS + n_gate],
        jnp.zeros((d, GATE_COLS - n_gate), F32)], axis=1).astype(BF16)
    bif_p = jnp.concatenate([b_if[l], jnp.zeros((GATE_COLS - n_gate,), F32)]).reshape(1, GATE_COLS)
    w_up_c = _chunk_cols(w_up[l]).astype(BF16)
    cw_c = _chunk_cols(conv_ffn_w[l])
    cb_c = _chunk_cols(conv_ffn_b[l].reshape(1, -1))
    w_down_c = w_down[l].reshape(N_FF_CHUNK, TC_FF, d).astype(BF16)
    w_out_b = w_out[l].astype(BF16)

    mod3 = _adaln(c, w_ada[l], b_ada[l]).reshape(bsz, 1, 6 * d)
    mm, dd, gates = _inproj(x, mod3, g_mix[l].reshape(1, d), w_in_p, cos_t, sin_t, bif_p)
    hm = _mlstm(mm, gates, conv_qk_w[l], conv_qk_b[l].reshape(1, -1),
                g_mlstm[l].reshape(1, -1), bsz, seq)
    hd = _diffattn(dd, lam_q1[l].reshape(1, -1), lam_k1[l].reshape(1, -1),
                   lam_q2[l].reshape(1, -1), lam_k2[l].reshape(1, -1),
                   g_diff[l].reshape(1, -1), bsz, seq, lam_init)
    return _ffn(x, hm, hd, mod3, g_ffn[l].reshape(1, d), g_final.reshape(1, d),
                w_out_b, w_up_c, cw_c, cb_c, w_down_c)
```

```python
import functools
import math

import jax
import jax.numpy as jnp
from jax import lax
from jax.experimental import pallas as pl
from jax.experimental.pallas import tpu as pltpu

F32 = jnp.float32
BF16 = jnp.bfloat16

D_MODEL = 1024
DEPTH = 1
W_M = D_MODEL // 2
H_M = 4
DH_M = W_M // H_M
CONV_QK = 4
W_D = D_MODEL - W_M
H_D = 4
DV_D = W_D // H_D
DQK_D = DV_D // 2
D_FF = ((8 * D_MODEL) // 3 + 127) // 128 * 128
CONV_FFN = 3
ROPE_THETA = 10000.0
EPS = 1e-6

LANES = 128
SUBLANES = 8
GATE_COLS = LANES
MM_COLS = 4 * W_M
DD_COLS = 3 * W_D

TM_IN = 512
L_M = 256
TQ = 256
TM_FFN = 512
MXU_DIM = 256
FF_CHUNKS = ((0, 6 * MXU_DIM), (6 * MXU_DIM, D_FF - 6 * MXU_DIM))
VMEM_LIMIT = 56 * 1024 * 1024


def _sigmoid(v):
    return 1.0 / (1.0 + jnp.exp(-v))


def _split3(v):
    hi = v.astype(BF16)
    r1 = v - hi.astype(F32)
    mid = r1.astype(BF16)
    lo = (r1 - mid.astype(F32)).astype(BF16)
    return hi, mid, lo


def _twice(v):
    return jnp.concatenate([v, v], axis=1)


def _adaln_kernel(c_ref, w_ref, b_ref, o_ref):
    c = c_ref[...]
    ca = (c * _sigmoid(c)).astype(BF16)
    o_ref[...] = jnp.dot(ca, w_ref[...].astype(BF16), preferred_element_type=F32) + b_ref[...]


def _adaln(c, w_ada, b_ada):
    bsz, d = c.shape
    n = w_ada.shape[1]
    tn = 1024
    return pl.pallas_call(
        _adaln_kernel,
        grid=(n // tn,),
        in_specs=[
            pl.BlockSpec((bsz, d), lambda j: (0, 0)),
            pl.BlockSpec((d, tn), lambda j: (0, j)),
            pl.BlockSpec((1, tn), lambda j: (0, j)),
        ],
        out_specs=pl.BlockSpec((bsz, tn), lambda j: (0, j)),
        out_shape=jax.ShapeDtypeStruct((bsz, n), F32),
        compiler_params=pltpu.CompilerParams(
            dimension_semantics=("arbitrary",), vmem_limit_bytes=VMEM_LIMIT),
        name="adaln",
    )(c, w_ada, b_ada.reshape(1, n))


def _shifted_rows(cur, prev, n_shift):
    ext = jnp.concatenate([prev, cur], axis=0)
    return [pltpu.roll(ext, sh, 0)[SUBLANES:, :] for sh in range(1, n_shift + 1)]


def _inproj_kernel(x_ref, sh_ref, sc_ref, g_ref, wmm_ref, wdd_ref, wg_ref, cos_ref, sin_ref, bif_ref,
                   cw_ref, cb_ref, mm_ref, dd_ref, gate_ref, carry_ref, *, tiles_per_seq):
    seq_start = (pl.program_id(0) % tiles_per_seq) == 0
    x = x_ref[...]
    tm = x.shape[0]
    ms = jnp.mean(x * x, axis=-1, keepdims=True)
    y = x * lax.rsqrt(ms + EPS) * g_ref[...]
    h = (y * (1.0 + sc_ref[...]) + sh_ref[...]).astype(BF16)
    pm = jnp.dot(h, wmm_ref[...], preferred_element_type=F32)
    mm_ref[:, 2 * W_M:] = pm[:, 2 * W_M:].astype(BF16)
    qk = pm[:, :2 * W_M]
    prev = jnp.where(seq_start, 0.0, carry_ref[...])
    carry_ref[...] = qk[tm - SUBLANES:, :]
    u = qk * cw_ref[CONV_QK - 1:CONV_QK, :] + cb_ref[...]
    for sh, shifted in enumerate(_shifted_rows(qk, prev, CONV_QK - 1), start=1):
        u = u + shifted * cw_ref[CONV_QK - 1 - sh:CONV_QK - sh, :]
    u = u * _sigmoid(u)
    mm_ref[:, :W_M] = u[:, :W_M].astype(BF16)
    mm_ref[:, W_M:2 * W_M] = (u[:, W_M:] * (DH_M ** -0.5)).astype(BF16)
    gate_ref[...] = jnp.dot(h, wg_ref[...], preferred_element_type=F32) + bif_ref[...]
    pd = jnp.dot(h, wdd_ref[...], preferred_element_type=F32)
    cos = cos_ref[...]
    sin = sin_ref[...]
    lane = lax.broadcasted_iota(jnp.int32, cos.shape, 1)
    first_half = (lane & (DQK_D // 2)) == 0
    for j in range(2 * W_D // LANES):
        xs = pd[:, j * LANES:(j + 1) * LANES]
        partner = jnp.where(first_half,
                            pltpu.roll(xs, LANES - DQK_D // 2, 1),
                            pltpu.roll(xs, DQK_D // 2, 1))
        dd_ref[:, j * LANES:(j + 1) * LANES] = (xs * cos + partner * sin).astype(BF16)
    dd_ref[:, 2 * W_D:] = pd[:, 2 * W_D:].astype(BF16)


def _inproj(x3, mod3, g_mix, w_mm, w_dd, w_g, cos_t, sin_t, bif_p, conv_w, conv_b):
    bsz, seq, d = x3.shape
    t = bsz * seq
    tiles_per_seq = seq // TM_IN

    def seq_spec(width):
        return pl.BlockSpec((None, TM_IN, width), lambda i: (i // tiles_per_seq, i % tiles_per_seq, 0))

    def const(shape):
        return pl.BlockSpec(shape, lambda i: (0,) * len(shape))

    return pl.pallas_call(
        functools.partial(_inproj_kernel, tiles_per_seq=tiles_per_seq),
        grid=(t // TM_IN,),
        in_specs=[
            seq_spec(d),
            pl.BlockSpec((None, 1, d), lambda i: (i // tiles_per_seq, 0, 0)),
            pl.BlockSpec((None, 1, d), lambda i: (i // tiles_per_seq, 0, 1)),
            const((1, d)),
            const((d, MM_COLS)), const((d, DD_COLS)), const((d, GATE_COLS)),
            seq_spec(LANES), seq_spec(LANES),
            const((1, GATE_COLS)),
            const((CONV_QK, 2 * W_M)), const((1, 2 * W_M)),
        ],
        out_specs=[
            pl.BlockSpec((TM_IN, MM_COLS), lambda i: (i, 0)),
            pl.BlockSpec((TM_IN, DD_COLS), lambda i: (i, 0)),
            pl.BlockSpec((TM_IN, GATE_COLS), lambda i: (i, 0)),
        ],
        out_shape=[
            jax.ShapeDtypeStruct((t, MM_COLS), BF16),
            jax.ShapeDtypeStruct((t, DD_COLS), BF16),
            jax.ShapeDtypeStruct((t, GATE_COLS), F32),
        ],
        scratch_shapes=[pltpu.VMEM((SUBLANES, 2 * W_M), F32)],
        compiler_params=pltpu.CompilerParams(
            dimension_semantics=("arbitrary",), vmem_limit_bytes=VMEM_LIMIT),
        name="inproj",
    )(x3, mod3, mod3, g_mix, w_mm, w_dd, w_g, cos_t, sin_t, bif_p, conv_w, conv_b)


def _mlstm_kernel(q_ref, k_ref, v_ref, o_ref, gate_ref, g_ref, out_ref, s_ref):
    seq = q_ref.shape[0]
    n_chunks = seq // L_M
    s_ref[...] = jnp.zeros_like(s_ref)
    row = lax.broadcasted_iota(jnp.int32, (L_M, L_M), 0)
    col = lax.broadcasted_iota(jnp.int32, (L_M, L_M), 1)
    tri = row >= col
    tri_b = tri.astype(BF16)
    lane = lax.broadcasted_iota(jnp.int32, (L_M, LANES), 1)
    ones_blk = jnp.ones((L_M, LANES), BF16)

    def body(c, m_prev):
        start = pl.multiple_of(c * L_M, L_M)
        qa = q_ref[pl.ds(start, L_M), :]
        ka = k_ref[pl.ds(start, L_M), :]
        va = v_ref[pl.ds(start, L_M), :]
        oa = o_ref[pl.ds(start, L_M), :].astype(F32)
        g = gate_ref[pl.ds(start, L_M), :]
        lf = jnp.minimum(g, 0.0) - jnp.log1p(jnp.exp(-jnp.abs(g)))
        hi, mid, lo = _split3(lf)
        b_all = (jnp.dot(tri_b, hi, preferred_element_type=F32)
                 + jnp.dot(tri_b, mid, preferred_element_type=F32)
                 + jnp.dot(tri_b, lo, preferred_element_type=F32))
        zt = jnp.where(lane < H_M, g, b_all).T
        m_out = []
        for hh in range(H_M):
            sl = slice(hh * DH_M, (hh + 1) * DH_M)
            i_rep = jnp.broadcast_to(g[:, hh:hh + 1], (L_M, LANES))
            b_rep = jnp.broadcast_to(b_all[:, H_M + hh:H_M + hh + 1], (L_M, LANES))
            r_row = zt[hh:hh + 1, :] - zt[H_M + hh:H_M + hh + 1, :]
            m_p = m_prev[hh]
            dmat = jnp.where(tri, _twice(b_rep) + r_row, -jnp.inf)
            inter = b_rep + m_p
            m_t = jnp.maximum(inter, jnp.max(dmat, axis=-1, keepdims=True))
            qh = qa[:, sl]
            kh = ka[:, sl]
            s_qk = lax.dot_general(qh, kh, (((1,), (1,)), ((), ())), preferred_element_type=F32)
            w_intra = (jnp.exp(dmat - _twice(m_t)) * s_qk).astype(BF16)
            w_inter = jnp.exp(inter - m_t)
            v_aug = jnp.concatenate([va[:, sl], ones_blk], axis=1)
            s_old = s_ref[hh]
            tot = (_twice(w_inter) * jnp.dot(qh, s_old.astype(BF16), preferred_element_type=F32)
                   + jnp.dot(w_intra, v_aug, preferred_element_type=F32))
            num = tot[:, :DH_M]
            den = tot[:, DH_M:]
            hval = num / jnp.maximum(jnp.abs(den), jnp.exp(-m_t))
            b_last = b_rep[L_M - 1:L_M, :]
            log_s = b_last - b_rep + i_rep
            m_new = jnp.maximum(b_last + m_p, jnp.max(log_s, axis=0, keepdims=True))
            w_s = jnp.exp(log_s - m_new)
            decay = jnp.exp(b_last + m_p - m_new)
            kw = (kh.astype(F32) * w_s).astype(BF16)
            s_ref[hh] = _twice(decay) * s_old + lax.dot_general(
                kw, v_aug, (((0,), (0,)), ((), ())), preferred_element_type=F32)
            m_out.append(m_new)
            hn = hval * lax.rsqrt(jnp.mean(hval * hval, axis=-1, keepdims=True) + EPS)
            out_ref[pl.ds(start, L_M), sl] = (
                hn * g_ref[:, sl] * _sigmoid(oa[:, sl])).astype(BF16)
        return tuple(m_out)

    m0 = tuple(jnp.zeros((1, LANES), F32) for _ in range(H_M))
    lax.fori_loop(0, n_chunks, body, m0)


def _mlstm(mm, gates, g_mlstm, bsz, seq):
    t = mm.shape[0]
    return pl.pallas_call(
        _mlstm_kernel,
        grid=(bsz,),
        in_specs=[
            pl.BlockSpec((seq, W_M), lambda b: (b, 0)),
            pl.BlockSpec((seq, W_M), lambda b: (b, 1)),
            pl.BlockSpec((seq, W_M), lambda b: (b, 2)),
            pl.BlockSpec((seq, W_M), lambda b: (b, 3)),
            pl.BlockSpec((seq, GATE_COLS), lambda b: (b, 0)),
            pl.BlockSpec((1, W_M), lambda b: (0, 0)),
        ],
        out_specs=pl.BlockSpec((seq, W_M), lambda b: (b, 0)),
        out_shape=jax.ShapeDtypeStruct((t, W_M), BF16),
        scratch_shapes=[pltpu.VMEM((H_M, DH_M, 2 * DH_M), F32)],
        compiler_params=pltpu.CompilerParams(
            dimension_semantics=("arbitrary",), vmem_limit_bytes=VMEM_LIMIT),
        name="mlstm",
    )(mm, mm, mm, mm, gates, g_mlstm)


def _diffattn_kernel(q_ref, k_ref, v_ref, lq1_ref, lk1_ref, lq2_ref, lk2_ref, g_ref,
                     out_ref, vaug_ref, *, lam_init):
    seq = q_ref.shape[0]
    lane = lax.broadcasted_iota(jnp.int32, (TQ, LANES), 1)
    comp0 = lane < DQK_D
    vaug_ref[:, :DV_D] = v_ref[...]
    vaug_ref[:, DV_D:] = jnp.ones((seq, DV_D), BF16)
    row = lax.broadcasted_iota(jnp.int32, (2 * TQ, TQ), 0)
    col = lax.broadcasted_iota(jnp.int32, (2 * TQ, TQ), 1)
    causal = col <= (row & (TQ - 1))
    lam = (jnp.exp(jnp.sum(lq1_ref[...] * lk1_ref[...], axis=-1, keepdims=True))
           - jnp.exp(jnp.sum(lq2_ref[...] * lk2_ref[...], axis=-1, keepdims=True)) + lam_init)
    for qi in range(seq // TQ):
        n_keys = (qi + 1) * TQ
        q = q_ref[qi * TQ:(qi + 1) * TQ, :]
        zero = jnp.zeros_like(q)
        q2 = jnp.concatenate([jnp.where(comp0, q, zero), jnp.where(comp0, zero, q)], axis=0)
        s_diag = lax.dot_general(q2, k_ref[n_keys - TQ:n_keys, :], (((1,), (1,)), ((), ())),
                                 preferred_element_type=F32)
        s_diag = jnp.where(causal, s_diag, -jnp.inf)
        m = jnp.max(s_diag, axis=-1, keepdims=True)
        if qi > 0:
            s_past = lax.dot_general(q2, k_ref[:n_keys - TQ, :], (((1,), (1,)), ((), ())),
                                     preferred_element_type=F32)
            m = jnp.maximum(m, jnp.max(s_past, axis=-1, keepdims=True))
            p = jnp.concatenate([jnp.exp(s_past - m).astype(BF16),
                                 jnp.exp(s_diag - m).astype(BF16)], axis=1)
        else:
            p = jnp.exp(s_diag - m).astype(BF16)
        a = jnp.dot(p, vaug_ref[:n_keys, :], preferred_element_type=F32)
        a0 = a[:TQ]
        a1 = a[TQ:]
        o = a0[:, :DV_D] / a0[:, DV_D:] - lam * (a1[:, :DV_D] / a1[:, DV_D:])
        on = o * lax.rsqrt(jnp.mean(o * o, axis=-1, keepdims=True) + EPS)
        out_ref[qi * TQ:(qi + 1) * TQ, :] = (on * g_ref[...] * (1.0 - lam_init)).astype(BF16)


def _diffattn(dd, lq1, lk1, lq2, lk2, g_diff, bsz, seq, lam_init):
    t = dd.shape[0]
    lam_spec = pl.BlockSpec((1, DQK_D), lambda b, h: (0, 0))
    return pl.pallas_call(
        functools.partial(_diffattn_kernel, lam_init=lam_init),
        grid=(bsz, H_D),
        in_specs=[
            pl.BlockSpec((seq, DV_D), lambda b, h: (b, h)),
            pl.BlockSpec((seq, DV_D), lambda b, h: (b, H_D + h)),
            pl.BlockSpec((seq, DV_D), lambda b, h: (b, 2 * H_D + h)),
            lam_spec, lam_spec, lam_spec, lam_spec,
            pl.BlockSpec((1, DV_D), lambda b, h: (0, h)),
        ],
        out_specs=pl.BlockSpec((seq, DV_D), lambda b, h: (b, h)),
        out_shape=jax.ShapeDtypeStruct((t, W_D), BF16),
        scratch_shapes=[pltpu.VMEM((seq, 2 * DV_D), BF16)],
        compiler_params=pltpu.CompilerParams(
            dimension_semantics=("arbitrary", "arbitrary"), vmem_limit_bytes=VMEM_LIMIT),
        name="diffattn",
    )(dd, dd, dd, lq1, lk1, lq2, lk2, g_diff)


def _ffn_kernel(x_ref, hm_ref, hd_ref, gta_ref, shf_ref, scf_ref, gtf_ref, gffn_ref, gfin_ref,
                wout_ref, wup_ref, cw_ref, cb_ref, wdown_ref, out_ref, carry_ref,
                *, tiles_per_seq):
    i = pl.program_id(0)
    seq_start = (i % tiles_per_seq) == 0
    mix = (jnp.dot(hm_ref[...], wout_ref[:W_M, :], preferred_element_type=F32)
           + jnp.dot(hd_ref[...], wout_ref[W_M:, :], preferred_element_type=F32))
    x1 = x_ref[...] + gta_ref[...] * mix
    y = x1 * lax.rsqrt(jnp.mean(x1 * x1, axis=-1, keepdims=True) + EPS) * gffn_ref[...]
    h2 = (y * (1.0 + scf_ref[...]) + shf_ref[...]).astype(BF16)
    tm = h2.shape[0]

    def conv_cols(col0, width):
        cols = slice(col0, col0 + width)
        p = jnp.dot(h2, wup_ref[:, cols], preferred_element_type=F32)
        prev = jnp.where(seq_start, 0.0, carry_ref[:, cols])
        carry_ref[:, cols] = p[tm - SUBLANES:, :]
        u = p * cw_ref[CONV_FFN - 1:CONV_FFN, cols] + cb_ref[:, cols]
        for sh, shifted in enumerate(_shifted_rows(p, prev, CONV_FFN - 1), start=1):
            u = u + shifted * cw_ref[CONV_FFN - 1 - sh:CONV_FFN - sh, cols]
        return u

    acc = jnp.zeros((tm, D_MODEL), F32)
    for c0, width in FF_CHUNKS:
        a = conv_cols(c0, width)
        g = conv_cols(D_FF + c0, width)
        act = (g * _sigmoid(g) * a).astype(BF16)
        acc = acc + jnp.dot(act, wdown_ref[c0:c0 + width, :], preferred_element_type=F32)
    x2 = x1 + gtf_ref[...] * acc
    out_ref[...] = x2 * lax.rsqrt(jnp.mean(x2 * x2, axis=-1, keepdims=True) + EPS) * gfin_ref[...]


def _ffn(x3, hm, hd, mod3, g_ffn, g_final, w_out_b, w_up_b, conv_w, conv_b, w_down_b):
    bsz, seq, d = x3.shape
    t = bsz * seq
    tiles_per_seq = seq // TM_FFN

    def mod_spec(j):
        return pl.BlockSpec((None, 1, d), lambda i: (i // tiles_per_seq, 0, j))

    def const(shape):
        return pl.BlockSpec(shape, lambda i: (0,) * len(shape), pipeline_mode=pl.Buffered(1))

    x_spec = pl.BlockSpec((None, TM_FFN, d), lambda i: (i // tiles_per_seq, i % tiles_per_seq, 0))
    return pl.pallas_call(
        functools.partial(_ffn_kernel, tiles_per_seq=tiles_per_seq),
        grid=(t // TM_FFN,),
        in_specs=[
            x_spec,
            pl.BlockSpec((TM_FFN, W_M), lambda i: (i, 0)),
            pl.BlockSpec((TM_FFN, W_D), lambda i: (i, 0)),
            mod_spec(2), mod_spec(3), mod_spec(4), mod_spec(5),
            const((1, d)), const((1, d)),
            const((d, d)),
            const((d, 2 * D_FF)),
            const((CONV_FFN, 2 * D_FF)),
            const((1, 2 * D_FF)),
            const((D_FF, d)),
        ],
        out_specs=x_spec,
        out_shape=jax.ShapeDtypeStruct((bsz, seq, d), F32),
        scratch_shapes=[pltpu.VMEM((SUBLANES, 2 * D_FF), F32)],
        compiler_params=pltpu.CompilerParams(
            dimension_semantics=("arbitrary",), vmem_limit_bytes=VMEM_LIMIT),
        name="ffn",
    )(x3, hm, hd, mod3, mod3, mod3, mod3, g_ffn, g_final, w_out_b, w_up_b, conv_w, conv_b, w_down_b)


def kernel(x, c, positions, w_ada, b_ada, g_mix, w_in, conv_qk_w, conv_qk_b, b_if, g_mlstm,
           lam_q1, lam_k1, lam_q2, lam_k2, g_diff, w_out, g_ffn, w_up, conv_ffn_w, conv_ffn_b,
           w_down, g_final):
    bsz, seq, d = x.shape
    assert w_ada.shape[0] == DEPTH == 1
    l = 0
    n_gate = 2 * H_M
    lam_init = 0.8 - 0.6 * math.exp(-0.3 * l)

    half = DQK_D // 2
    inv_freq = ROPE_THETA ** (-jnp.arange(half, dtype=F32) / half)
    ang = positions.astype(F32)[..., None] * jnp.tile(inv_freq, LANES // half)
    sign = jnp.tile(jnp.concatenate([-jnp.ones((half,), F32), jnp.ones((half,), F32)]), LANES // DQK_D)
    cos_t = jnp.cos(ang)
    sin_t = jnp.sin(ang) * sign

    w_l = w_in[l]
    w_mm = w_l[:, :MM_COLS].astype(BF16)
    q_scale = jnp.concatenate([jnp.full((W_D,), DQK_D ** -0.5, F32), jnp.ones((2 * W_D,), F32)])
    w_dd = (w_l[:, MM_COLS + n_gate:] * q_scale).astype(BF16)
    w_g = jnp.pad(w_l[:, MM_COLS:MM_COLS + n_gate], ((0, 0), (0, GATE_COLS - n_gate))).astype(BF16)
    bif_p = jnp.pad(b_if[l], (0, GATE_COLS - n_gate)).reshape(1, GATE_COLS)

    mod3 = _adaln(c, w_ada[l], b_ada[l]).reshape(bsz, 1, 6 * d)
    mm, dd, gates = _inproj(x, mod3, g_mix[l].reshape(1, d), w_mm, w_dd, w_g, cos_t, sin_t, bif_p,
                            conv_qk_w[l], conv_qk_b[l].reshape(1, -1))
    hm = _mlstm(mm, gates, g_mlstm[l].reshape(1, -1), bsz, seq)
    hd = _diffattn(dd, lam_q1[l].reshape(1, -1), lam_k1[l].reshape(1, -1),
                   lam_q2[l].reshape(1, -1), lam_k2[l].reshape(1, -1),
                   g_diff[l].reshape(1, -1), bsz, seq, lam_init)
    return _ffn(x, hm, hd, mod3, g_ffn[l].reshape(1, d), g_final.reshape(1, d),
                w_out[l].astype(BF16), w_up[l].astype(BF16), conv_ffn_w[l],
                conv_ffn_b[l].reshape(1, -1), w_down[l].astype(BF16))
```

```python
import functools
import math

import jax
import jax.numpy as jnp
from jax import lax
from jax.experimental import pallas as pl
from jax.experimental.pallas import tpu as pltpu

F32 = jnp.float32
BF16 = jnp.bfloat16

D_MODEL = 1024
DEPTH = 1
W_M = D_MODEL // 2
H_M = 4
DH_M = W_M // H_M
CONV_QK = 4
W_D = D_MODEL - W_M
H_D = 4
DV_D = W_D // H_D
DQK_D = DV_D // 2
D_FF = ((8 * D_MODEL) // 3 + 127) // 128 * 128
CONV_FFN = 3
ROPE_THETA = 10000.0
EPS = 1e-6

LANES = 128
SUBLANES = 8
GATE_COLS = LANES
MM_COLS = 4 * W_M
DD_COLS = 3 * W_D

TM_IN = 512
L_M = 256
TQ = 256
TM_FFN = 512
MXU_DIM = 256
FF_CHUNKS = ((0, 6 * MXU_DIM), (6 * MXU_DIM, D_FF - 6 * MXU_DIM))
VMEM_LIMIT = 56 * 1024 * 1024


def _sigmoid(v):
    return 1.0 / (1.0 + jnp.exp(-v))


def _split3(v):
    hi = v.astype(BF16)
    r1 = v - hi.astype(F32)
    mid = r1.astype(BF16)
    lo = (r1 - mid.astype(F32)).astype(BF16)
    return hi, mid, lo


def _twice(v):
    return jnp.concatenate([v, v], axis=1)


def _adaln_kernel(c_ref, w_ref, b_ref, o_ref):
    c = c_ref[...]
    ca = (c * _sigmoid(c)).astype(BF16)
    o_ref[...] = jnp.dot(ca, w_ref[...].astype(BF16), preferred_element_type=F32) + b_ref[...]


def _adaln(c, w_ada, b_ada):
    bsz, d = c.shape
    n = w_ada.shape[1]
    tn = 1024
    return pl.pallas_call(
        _adaln_kernel,
        grid=(n // tn,),
        in_specs=[
            pl.BlockSpec((bsz, d), lambda j: (0, 0)),
            pl.BlockSpec((d, tn), lambda j: (0, j)),
            pl.BlockSpec((1, tn), lambda j: (0, j)),
        ],
        out_specs=pl.BlockSpec((bsz, tn), lambda j: (0, j)),
        out_shape=jax.ShapeDtypeStruct((bsz, n), F32),
        compiler_params=pltpu.CompilerParams(
            dimension_semantics=("arbitrary",), vmem_limit_bytes=VMEM_LIMIT),
        name="adaln",
    )(c, w_ada, b_ada.reshape(1, n))


def _shifted_rows(cur, prev, n_shift):
    ext = jnp.concatenate([prev, cur], axis=0)
    return [pltpu.roll(ext, sh, 0)[SUBLANES:, :] for sh in range(1, n_shift + 1)]


def _inproj_kernel(x_ref, sh_ref, sc_ref, g_ref, wmm_ref, wdd_ref, wg_ref, rope_ref, bif_ref,
                   cw_ref, cb_ref, mm_ref, dd_ref, gate_ref, carry_ref, *, tiles_per_seq):
    seq_start = (pl.program_id(0) % tiles_per_seq) == 0
    x = x_ref[...]
    tm = x.shape[0]
    ms = jnp.mean(x * x, axis=-1, keepdims=True)
    y = x * lax.rsqrt(ms + EPS) * g_ref[...]
    h = (y * (1.0 + sc_ref[...]) + sh_ref[...]).astype(BF16)
    pm = jnp.dot(h, wmm_ref[...], preferred_element_type=F32)
    mm_ref[:, 2 * W_M:] = pm[:, 2 * W_M:].astype(BF16)
    qk = pm[:, :2 * W_M]
    prev = jnp.where(seq_start, 0.0, carry_ref[...])
    carry_ref[...] = qk[tm - SUBLANES:, :]
    u = qk * cw_ref[CONV_QK - 1:CONV_QK, :] + cb_ref[...]
    for sh, shifted in enumerate(_shifted_rows(qk, prev, CONV_QK - 1), start=1):
        u = u + shifted * cw_ref[CONV_QK - 1 - sh:CONV_QK - sh, :]
    u = u * _sigmoid(u)
    mm_ref[:, :W_M] = u[:, :W_M].astype(BF16)
    mm_ref[:, W_M:2 * W_M] = (u[:, W_M:] * (DH_M ** -0.5)).astype(BF16)
    gate_ref[...] = jnp.dot(h, wg_ref[...], preferred_element_type=F32) + bif_ref[...]
    pd = jnp.dot(h, wdd_ref[...], preferred_element_type=F32)
    cs = rope_ref[...]
    c32 = cs[:, :DQK_D // 2]
    s32 = cs[:, DQK_D // 2:]
    cos = jnp.concatenate([c32, c32, c32, c32], axis=1)
    sin = jnp.concatenate([-s32, s32, -s32, s32], axis=1)
    lane = lax.broadcasted_iota(jnp.int32, cos.shape, 1)
    first_half = (lane & (DQK_D // 2)) == 0
    for j in range(2 * W_D // LANES):
        xs = pd[:, j * LANES:(j + 1) * LANES]
        partner = jnp.where(first_half,
                            pltpu.roll(xs, LANES - DQK_D // 2, 1),
                            pltpu.roll(xs, DQK_D // 2, 1))
        dd_ref[:, j * LANES:(j + 1) * LANES] = (xs * cos + partner * sin).astype(BF16)
    dd_ref[:, 2 * W_D:] = pd[:, 2 * W_D:].astype(BF16)


def _inproj(x3, mod3, g_mix, w_mm, w_dd, w_g, rope_t, bif_p, conv_w, conv_b):
    bsz, seq, d = x3.shape
    t = bsz * seq
    tiles_per_seq = seq // TM_IN

    def seq_spec(width):
        return pl.BlockSpec((None, TM_IN, width), lambda i: (i // tiles_per_seq, i % tiles_per_seq, 0))

    def const(shape):
        return pl.BlockSpec(shape, lambda i: (0,) * len(shape))

    return pl.pallas_call(
        functools.partial(_inproj_kernel, tiles_per_seq=tiles_per_seq),
        grid=(t // TM_IN,),
        in_specs=[
            seq_spec(d),
            pl.BlockSpec((None, 1, d), lambda i: (i // tiles_per_seq, 0, 0)),
            pl.BlockSpec((None, 1, d), lambda i: (i // tiles_per_seq, 0, 1)),
            const((1, d)),
            const((d, MM_COLS)), const((d, DD_COLS)), const((d, GATE_COLS)),
            seq_spec(DQK_D),
            const((1, GATE_COLS)),
            const((CONV_QK, 2 * W_M)), const((1, 2 * W_M)),
        ],
        out_specs=[
            pl.BlockSpec((TM_IN, MM_COLS), lambda i: (i, 0)),
            pl.BlockSpec((TM_IN, DD_COLS), lambda i: (i, 0)),
            pl.BlockSpec((TM_IN, GATE_COLS), lambda i: (i, 0)),
        ],
        out_shape=[
            jax.ShapeDtypeStruct((t, MM_COLS), BF16),
            jax.ShapeDtypeStruct((t, DD_COLS), BF16),
            jax.ShapeDtypeStruct((t, GATE_COLS), F32),
        ],
        scratch_shapes=[pltpu.VMEM((SUBLANES, 2 * W_M), F32)],
        compiler_params=pltpu.CompilerParams(
            dimension_semantics=("arbitrary",), vmem_limit_bytes=VMEM_LIMIT),
        name="inproj",
    )(x3, mod3, mod3, g_mix, w_mm, w_dd, w_g, rope_t, bif_p, conv_w, conv_b)


def _mlstm_kernel(q_ref, k_ref, v_ref, o_ref, gate_ref, g_ref, out_ref, s_ref):
    seq = q_ref.shape[0]
    n_chunks = seq // L_M
    s_ref[...] = jnp.zeros_like(s_ref)
    row = lax.broadcasted_iota(jnp.int32, (L_M, L_M), 0)
    col = lax.broadcasted_iota(jnp.int32, (L_M, L_M), 1)
    tri = row >= col
    tri_b = tri.astype(BF16)
    lane = lax.broadcasted_iota(jnp.int32, (L_M, LANES), 1)
    ones_blk = jnp.ones((L_M, LANES), BF16)

    def body(c, m_prev):
        start = pl.multiple_of(c * L_M, L_M)
        qa = q_ref[pl.ds(start, L_M), :]
        ka = k_ref[pl.ds(start, L_M), :]
        va = v_ref[pl.ds(start, L_M), :]
        oa = o_ref[pl.ds(start, L_M), :].astype(F32)
        g = gate_ref[pl.ds(start, L_M), :]
        lf = jnp.minimum(g, 0.0) - jnp.log1p(jnp.exp(-jnp.abs(g)))
        hi, mid, lo = _split3(lf)
        b_all = (jnp.dot(tri_b, hi, preferred_element_type=F32)
                 + jnp.dot(tri_b, mid, preferred_element_type=F32)
                 + jnp.dot(tri_b, lo, preferred_element_type=F32))
        zt = jnp.where(lane < H_M, g, b_all).T
        m_out = []
        for hh in range(H_M):
            sl = slice(hh * DH_M, (hh + 1) * DH_M)
            i_rep = jnp.broadcast_to(g[:, hh:hh + 1], (L_M, LANES))
            b_rep = jnp.broadcast_to(b_all[:, H_M + hh:H_M + hh + 1], (L_M, LANES))
            r_row = zt[hh:hh + 1, :] - zt[H_M + hh:H_M + hh + 1, :]
            m_p = m_prev[hh]
            dmat = jnp.where(tri, _twice(b_rep) + r_row, -jnp.inf)
            inter = b_rep + m_p
            m_t = jnp.maximum(inter, jnp.max(dmat, axis=-1, keepdims=True))
            qh = qa[:, sl]
            kh = ka[:, sl]
            s_qk = lax.dot_general(qh, kh, (((1,), (1,)), ((), ())), preferred_element_type=F32)
            w_intra = (jnp.exp(dmat - _twice(m_t)) * s_qk).astype(BF16)
            w_inter = jnp.exp(inter - m_t)
            v_aug = jnp.concatenate([va[:, sl], ones_blk], axis=1)
            s_old = s_ref[hh]
            tot = (_twice(w_inter) * jnp.dot(qh, s_old.astype(BF16), preferred_element_type=F32)
                   + jnp.dot(w_intra, v_aug, preferred_element_type=F32))
            num = tot[:, :DH_M]
            den = tot[:, DH_M:]
            hval = num / jnp.maximum(jnp.abs(den), jnp.exp(-m_t))
            b_last = b_rep[L_M - 1:L_M, :]
            log_s = b_last - b_rep + i_rep
            m_new = jnp.maximum(b_last + m_p, jnp.max(log_s, axis=0, keepdims=True))
            w_s = jnp.exp(log_s - m_new)
            decay = jnp.exp(b_last + m_p - m_new)
            kw = (kh.astype(F32) * w_s).astype(BF16)
            s_ref[hh] = _twice(decay) * s_old + lax.dot_general(
                kw, v_aug, (((0,), (0,)), ((), ())), preferred_element_type=F32)
            m_out.append(m_new)
            hn = hval * lax.rsqrt(jnp.mean(hval * hval, axis=-1, keepdims=True) + EPS)
            out_ref[pl.ds(start, L_M), sl] = (
                hn * g_ref[:, sl] * _sigmoid(oa[:, sl])).astype(BF16)
        return tuple(m_out)

    m0 = tuple(jnp.zeros((1, LANES), F32) for _ in range(H_M))
    lax.fori_loop(0, n_chunks, body, m0)


def _mlstm(mm, gates, g_mlstm, bsz, seq):
    t = mm.shape[0]
    return pl.pallas_call(
        _mlstm_kernel,
        grid=(bsz,),
        in_specs=[
            pl.BlockSpec((seq, W_M), lambda b: (b, 0)),
            pl.BlockSpec((seq, W_M), lambda b: (b, 1)),
            pl.BlockSpec((seq, W_M), lambda b: (b, 2)),
            pl.BlockSpec((seq, W_M), lambda b: (b, 3)),
            pl.BlockSpec((seq, GATE_COLS), lambda b: (b, 0)),
            pl.BlockSpec((1, W_M), lambda b: (0, 0)),
        ],
        out_specs=pl.BlockSpec((seq, W_M), lambda b: (b, 0)),
        out_shape=jax.ShapeDtypeStruct((t, W_M), BF16),
        scratch_shapes=[pltpu.VMEM((H_M, DH_M, 2 * DH_M), F32)],
        compiler_params=pltpu.CompilerParams(
            dimension_semantics=("arbitrary",), vmem_limit_bytes=VMEM_LIMIT),
        name="mlstm",
    )(mm, mm, mm, mm, gates, g_mlstm)


def _diffattn_kernel(q_ref, k_ref, v_ref, lq1_ref, lk1_ref, lq2_ref, lk2_ref, g_ref,
                     out_ref, vaug_ref, *, lam_init):
    seq = q_ref.shape[0]
    lane = lax.broadcasted_iota(jnp.int32, (TQ, LANES), 1)
    comp0 = lane < DQK_D
    vaug_ref[:, :DV_D] = v_ref[...]
    vaug_ref[:, DV_D:] = jnp.ones((seq, DV_D), BF16)
    row = lax.broadcasted_iota(jnp.int32, (2 * TQ, TQ), 0)
    col = lax.broadcasted_iota(jnp.int32, (2 * TQ, TQ), 1)
    causal = col <= (row & (TQ - 1))
    lam = (jnp.exp(jnp.sum(lq1_ref[...] * lk1_ref[...], axis=-1, keepdims=True))
           - jnp.exp(jnp.sum(lq2_ref[...] * lk2_ref[...], axis=-1, keepdims=True)) + lam_init)
    for qi in reversed(range(seq // TQ)):
        n_keys = (qi + 1) * TQ
        q = q_ref[qi * TQ:(qi + 1) * TQ, :]
        zero = jnp.zeros_like(q)
        q2 = jnp.concatenate([jnp.where(comp0, q, zero), jnp.where(comp0, zero, q)], axis=0)
        s_diag = lax.dot_general(q2, k_ref[n_keys - TQ:n_keys, :], (((1,), (1,)), ((), ())),
                                 preferred_element_type=F32)
        s_diag = jnp.where(causal, s_diag, -jnp.inf)
        m = jnp.max(s_diag, axis=-1, keepdims=True)
        if qi > 0:
            s_past = lax.dot_general(q2, k_ref[:n_keys - TQ, :], (((1,), (1,)), ((), ())),
                                     preferred_element_type=F32)
            m = jnp.maximum(m, jnp.max(s_past, axis=-1, keepdims=True))
            p = jnp.concatenate([jnp.exp((s_past - m).astype(BF16)),
                                 jnp.exp((s_diag - m).astype(BF16))], axis=1)
        else:
            p = jnp.exp((s_diag - m).astype(BF16))
        a = jnp.dot(p, vaug_ref[:n_keys, :], preferred_element_type=F32)
        a0 = a[:TQ]
        a1 = a[TQ:]
        o = a0[:, :DV_D] / a0[:, DV_D:] - lam * (a1[:, :DV_D] / a1[:, DV_D:])
        on = o * lax.rsqrt(jnp.mean(o * o, axis=-1, keepdims=True) + EPS)
        out_ref[qi * TQ:(qi + 1) * TQ, :] = (on * g_ref[...] * (1.0 - lam_init)).astype(BF16)


def _diffattn(dd, lq1, lk1, lq2, lk2, g_diff, bsz, seq, lam_init):
    t = dd.shape[0]
    lam_spec = pl.BlockSpec((1, DQK_D), lambda b, h: (0, 0))
    return pl.pallas_call(
        functools.partial(_diffattn_kernel, lam_init=lam_init),
        grid=(bsz, H_D),
        in_specs=[
            pl.BlockSpec((seq, DV_D), lambda b, h: (b, h)),
            pl.BlockSpec((seq, DV_D), lambda b, h: (b, H_D + h)),
            pl.BlockSpec((seq, DV_D), lambda b, h: (b, 2 * H_D + h)),
            lam_spec, lam_spec, lam_spec, lam_spec,
            pl.BlockSpec((1, DV_D), lambda b, h: (0, h)),
        ],
        out_specs=pl.BlockSpec((seq, DV_D), lambda b, h: (b, h)),
        out_shape=jax.ShapeDtypeStruct((t, W_D), BF16),
        scratch_shapes=[pltpu.VMEM((seq, 2 * DV_D), BF16)],
        compiler_params=pltpu.CompilerParams(
            dimension_semantics=("arbitrary", "arbitrary"), vmem_limit_bytes=VMEM_LIMIT),
        name="diffattn",
    )(dd, dd, dd, lq1, lk1, lq2, lk2, g_diff)


def _ffn_kernel(x_ref, hm_ref, hd_ref, gta_ref, shf_ref, scf_ref, gtf_ref, gffn_ref, gfin_ref,
                wout_ref, wup_ref, cw_ref, cb_ref, wdown_ref, out_ref, carry_ref,
                *, tiles_per_seq):
    i = pl.program_id(0)
    seq_start = (i % tiles_per_seq) == 0
    mix = (jnp.dot(hm_ref[...], wout_ref[:W_M, :], preferred_element_type=F32)
           + jnp.dot(hd_ref[...], wout_ref[W_M:, :], preferred_element_type=F32))
    x1 = x_ref[...] + gta_ref[...] * mix
    y = x1 * lax.rsqrt(jnp.mean(x1 * x1, axis=-1, keepdims=True) + EPS) * gffn_ref[...]
    h2 = (y * (1.0 + scf_ref[...]) + shf_ref[...]).astype(BF16)
    tm = h2.shape[0]

    def conv_cols(col0, width):
        cols = slice(col0, col0 + width)
        p = jnp.dot(h2, wup_ref[:, cols], preferred_element_type=F32)
        prev = jnp.where(seq_start, 0.0, carry_ref[:, cols])
        carry_ref[:, cols] = p[tm - SUBLANES:, :]
        u = p * cw_ref[CONV_FFN - 1:CONV_FFN, cols] + cb_ref[:, cols]
        for sh, shifted in enumerate(_shifted_rows(p, prev, CONV_FFN - 1), start=1):
            u = u + shifted * cw_ref[CONV_FFN - 1 - sh:CONV_FFN - sh, cols]
        return u

    acc = jnp.zeros((tm, D_MODEL), F32)
    for c0, width in FF_CHUNKS:
        a = conv_cols(c0, width)
        g = conv_cols(D_FF + c0, width)
        act = (g * _sigmoid(g) * a).astype(BF16)
        acc = acc + jnp.dot(act, wdown_ref[c0:c0 + width, :], preferred_element_type=F32)
    x2 = x1 + gtf_ref[...] * acc
    out_ref[...] = x2 * lax.rsqrt(jnp.mean(x2 * x2, axis=-1, keepdims=True) + EPS) * gfin_ref[...]


def _ffn(x3, hm, hd, mod3, g_ffn, g_final, w_out_b, w_up_b, conv_w, conv_b, w_down_b):
    bsz, seq, d = x3.shape
    t = bsz * seq
    tiles_per_seq = seq // TM_FFN

    def mod_spec(j):
        return pl.BlockSpec((None, 1, d), lambda i: (i // tiles_per_seq, 0, j))

    def const(shape):
        return pl.BlockSpec(shape, lambda i: (0,) * len(shape), pipeline_mode=pl.Buffered(1))

    x_spec = pl.BlockSpec((None, TM_FFN, d), lambda i: (i // tiles_per_seq, i % tiles_per_seq, 0))
    return pl.pallas_call(
        functools.partial(_ffn_kernel, tiles_per_seq=tiles_per_seq),
        grid=(t // TM_FFN,),
        in_specs=[
            x_spec,
            pl.BlockSpec((TM_FFN, W_M), lambda i: (i, 0)),
            pl.BlockSpec((TM_FFN, W_D), lambda i: (i, 0)),
            mod_spec(2), mod_spec(3), mod_spec(4), mod_spec(5),
            const((1, d)), const((1, d)),
            const((d, d)),
            const((d, 2 * D_FF)),
            const((CONV_FFN, 2 * D_FF)),
            const((1, 2 * D_FF)),
            const((D_FF, d)),
        ],
        out_specs=x_spec,
        out_shape=jax.ShapeDtypeStruct((bsz, seq, d), F32),
        scratch_shapes=[pltpu.VMEM((SUBLANES, 2 * D_FF), F32)],
        compiler_params=pltpu.CompilerParams(
            dimension_semantics=("arbitrary",), vmem_limit_bytes=VMEM_LIMIT),
        name="ffn",
    )(x3, hm, hd, mod3, mod3, mod3, mod3, g_ffn, g_final, w_out_b, w_up_b, conv_w, conv_b, w_down_b)


def kernel(x, c, positions, w_ada, b_ada, g_mix, w_in, conv_qk_w, conv_qk_b, b_if, g_mlstm,
           lam_q1, lam_k1, lam_q2, lam_k2, g_diff, w_out, g_ffn, w_up, conv_ffn_w, conv_ffn_b,
           w_down, g_final):
    bsz, seq, d = x.shape
    assert w_ada.shape[0] == DEPTH == 1
    l = 0
    n_gate = 2 * H_M
    lam_init = 0.8 - 0.6 * math.exp(-0.3 * l)

    half = DQK_D // 2
    inv_freq = ROPE_THETA ** (-jnp.arange(half, dtype=F32) / half)
    ang = positions.astype(F32)[..., None] * inv_freq
    rope_t = jnp.concatenate([jnp.cos(ang), jnp.sin(ang)], axis=-1)

    w_l = w_in[l]
    w_mm = w_l[:, :MM_COLS].astype(BF16)
    q_scale = jnp.concatenate([jnp.full((W_D,), DQK_D ** -0.5, F32), jnp.ones((2 * W_D,), F32)])
    w_dd = (w_l[:, MM_COLS + n_gate:] * q_scale).astype(BF16)
    w_g = jnp.pad(w_l[:, MM_COLS:MM_COLS + n_gate], ((0, 0), (0, GATE_COLS - n_gate))).astype(BF16)
    bif_p = jnp.pad(b_if[l], (0, GATE_COLS - n_gate)).reshape(1, GATE_COLS)

    mod3 = _adaln(c, w_ada[l], b_ada[l]).reshape(bsz, 1, 6 * d)
    mm, dd, gates = _inproj(x, mod3, g_mix[l].reshape(1, d), w_mm, w_dd, w_g, rope_t, bif_p,
                            conv_qk_w[l], conv_qk_b[l].reshape(1, -1))
    hm = _mlstm(mm, gates, g_mlstm[l].reshape(1, -1), bsz, seq)
    hd = _diffattn(dd, lam_q1[l].reshape(1, -1), lam_k1[l].reshape(1, -1),
                   lam_q2[l].reshape(1, -1), lam_k2[l].reshape(1, -1),
                   g_diff[l].reshape(1, -1), bsz, seq, lam_init)
    return _ffn(x, hm, hd, mod3, g_ffn[l].reshape(1, d), g_final.reshape(1, d),
                w_out[l].astype(BF16), w_up[l].astype(BF16), conv_ffn_w[l],
                conv_ffn_b[l].reshape(1, -1), w_down[l].astype(BF16))
```

```python
import functools
import math

import jax
import jax.numpy as jnp
from jax import lax
from jax.experimental import pallas as pl
from jax.experimental.pallas import tpu as pltpu

F32 = jnp.float32
BF16 = jnp.bfloat16

D_MODEL = 1024
DEPTH = 1
W_M = D_MODEL // 2
H_M = 4
DH_M = W_M // H_M
CONV_QK = 4
W_D = D_MODEL - W_M
H_D = 4
DV_D = W_D // H_D
DQK_D = DV_D // 2
D_FF = ((8 * D_MODEL) // 3 + 127) // 128 * 128
CONV_FFN = 3
ROPE_THETA = 10000.0
EPS = 1e-6

LANES = 128
SUBLANES = 8
GATE_COLS = LANES
MM_COLS = 4 * W_M
DD_COLS = 3 * W_D

TM_IN = 512
L_M = 256
TQ = 256
TM_FFN = 512
MXU_DIM = 256
FF_CHUNKS = ((0, 6 * MXU_DIM), (6 * MXU_DIM, D_FF - 6 * MXU_DIM))
VMEM_LIMIT = 56 * 1024 * 1024
NT_DIMS = (((1,), (1,)), ((), ()))


def _sigmoid(v):
    return 1.0 / (1.0 + jnp.exp(-v))


def _split3(v):
    hi = v.astype(BF16)
    r1 = v - hi.astype(F32)
    mid = r1.astype(BF16)
    lo = (r1 - mid.astype(F32)).astype(BF16)
    return hi, mid, lo


def _twice(v):
    return jnp.concatenate([v, v], axis=1)


def _adaln_kernel(c_ref, w_ref, b_ref, o_ref):
    c = c_ref[...]
    ca = (c * _sigmoid(c)).astype(BF16)
    o_ref[...] = jnp.dot(ca, w_ref[...].astype(BF16), preferred_element_type=F32) + b_ref[...]


def _adaln(c, w_ada, b_ada):
    bsz, d = c.shape
    n = w_ada.shape[1]
    tn = 1024
    return pl.pallas_call(
        _adaln_kernel,
        grid=(n // tn,),
        in_specs=[
            pl.BlockSpec((bsz, d), lambda j: (0, 0)),
            pl.BlockSpec((d, tn), lambda j: (0, j)),
            pl.BlockSpec((1, tn), lambda j: (0, j)),
        ],
        out_specs=pl.BlockSpec((bsz, tn), lambda j: (0, j)),
        out_shape=jax.ShapeDtypeStruct((bsz, n), F32),
        compiler_params=pltpu.CompilerParams(
            dimension_semantics=("arbitrary",), vmem_limit_bytes=VMEM_LIMIT),
        name="adaln",
    )(c, w_ada, b_ada.reshape(1, n))


def _shifted_rows(cur, prev, n_shift):
    ext = jnp.concatenate([prev, cur], axis=0)
    return [pltpu.roll(ext, sh, 0)[SUBLANES:, :] for sh in range(1, n_shift + 1)]


def _inproj_kernel(x_ref, sh_ref, sc_ref, g_ref, wmm_ref, wdd_ref, wg_ref, rope_ref, bif_ref,
                   cw_ref, cb_ref, mm_ref, dd_ref, gate_ref, carry_ref, *, tiles_per_seq):
    seq_start = (pl.program_id(0) % tiles_per_seq) == 0
    x = x_ref[...]
    tm = x.shape[0]
    ms = jnp.mean(x * x, axis=-1, keepdims=True)
    y = x * lax.rsqrt(ms + EPS) * g_ref[...]
    h = (y * (1.0 + sc_ref[...]) + sh_ref[...]).astype(BF16)
    pm = lax.dot_general(h, wmm_ref[...], NT_DIMS, preferred_element_type=F32)
    mm_ref[:, 2 * W_M:] = pm[:, 2 * W_M:].astype(BF16)
    qk = pm[:, :2 * W_M]
    prev = jnp.where(seq_start, 0.0, carry_ref[...])
    carry_ref[...] = qk[tm - SUBLANES:, :]
    u = qk * cw_ref[CONV_QK - 1:CONV_QK, :] + cb_ref[...]
    for sh, shifted in enumerate(_shifted_rows(qk, prev, CONV_QK - 1), start=1):
        u = u + shifted * cw_ref[CONV_QK - 1 - sh:CONV_QK - sh, :]
    u = u * _sigmoid(u)
    mm_ref[:, :W_M] = u[:, :W_M].astype(BF16)
    mm_ref[:, W_M:2 * W_M] = (u[:, W_M:] * (DH_M ** -0.5)).astype(BF16)
    gate_ref[...] = lax.dot_general(h, wg_ref[...], NT_DIMS, preferred_element_type=F32) + bif_ref[...]
    pd = lax.dot_general(h, wdd_ref[...], NT_DIMS, preferred_element_type=F32)
    cs = rope_ref[...].T
    c32 = cs[:, :DQK_D // 2]
    s32 = cs[:, DQK_D // 2:DQK_D]
    cos = jnp.concatenate([c32, c32, c32, c32], axis=1)
    sin = jnp.concatenate([-s32, s32, -s32, s32], axis=1)
    lane = lax.broadcasted_iota(jnp.int32, cos.shape, 1)
    first_half = (lane & (DQK_D // 2)) == 0
    for j in range(2 * W_D // LANES):
        xs = pd[:, j * LANES:(j + 1) * LANES]
        partner = jnp.where(first_half,
                            pltpu.roll(xs, LANES - DQK_D // 2, 1),
                            pltpu.roll(xs, DQK_D // 2, 1))
        dd_ref[:, j * LANES:(j + 1) * LANES] = (xs * cos + partner * sin).astype(BF16)
    dd_ref[:, 2 * W_D:] = pd[:, 2 * W_D:].astype(BF16)


def _inproj(x3, mod3, g_mix, w_mm, w_dd, w_g, rope_t, bif_p, conv_w, conv_b):
    bsz, seq, d = x3.shape
    t = bsz * seq
    tiles_per_seq = seq // TM_IN

    def seq_spec(width):
        return pl.BlockSpec((None, TM_IN, width), lambda i: (i // tiles_per_seq, i % tiles_per_seq, 0))

    def const(shape):
        return pl.BlockSpec(shape, lambda i: (0,) * len(shape))

    return pl.pallas_call(
        functools.partial(_inproj_kernel, tiles_per_seq=tiles_per_seq),
        grid=(t // TM_IN,),
        in_specs=[
            seq_spec(d),
            pl.BlockSpec((None, 1, d), lambda i: (i // tiles_per_seq, 0, 0)),
            pl.BlockSpec((None, 1, d), lambda i: (i // tiles_per_seq, 0, 1)),
            const((1, d)),
            const((MM_COLS, d)), const((DD_COLS, d)), const((GATE_COLS, d)),
            pl.BlockSpec((LANES, TM_IN), lambda i: (0, i)),
            const((1, GATE_COLS)),
            const((CONV_QK, 2 * W_M)), const((1, 2 * W_M)),
        ],
        out_specs=[
            pl.BlockSpec((TM_IN, MM_COLS), lambda i: (i, 0)),
            pl.BlockSpec((TM_IN, DD_COLS), lambda i: (i, 0)),
            pl.BlockSpec((TM_IN, GATE_COLS), lambda i: (i, 0)),
        ],
        out_shape=[
            jax.ShapeDtypeStruct((t, MM_COLS), BF16),
            jax.ShapeDtypeStruct((t, DD_COLS), BF16),
            jax.ShapeDtypeStruct((t, GATE_COLS), F32),
        ],
        scratch_shapes=[pltpu.VMEM((SUBLANES, 2 * W_M), F32)],
        compiler_params=pltpu.CompilerParams(
            dimension_semantics=("arbitrary",), vmem_limit_bytes=VMEM_LIMIT),
        name="inproj",
    )(x3, mod3, mod3, g_mix, w_mm, w_dd, w_g, rope_t, bif_p, conv_w, conv_b)


def _mlstm_kernel(q_ref, k_ref, v_ref, o_ref, gate_ref, g_ref, out_ref, s_ref):
    seq = q_ref.shape[0]
    n_chunks = seq // L_M
    s_ref[...] = jnp.zeros_like(s_ref)
    row = lax.broadcasted_iota(jnp.int32, (L_M, L_M), 0)
    col = lax.broadcasted_iota(jnp.int32, (L_M, L_M), 1)
    tri = row >= col
    tri_b = tri.astype(BF16)
    lane = lax.broadcasted_iota(jnp.int32, (L_M, LANES), 1)
    ones_blk = jnp.ones((L_M, LANES), BF16)

    def body(c, m_prev):
        start = pl.multiple_of(c * L_M, L_M)
        qa = q_ref[pl.ds(start, L_M), :]
        ka = k_ref[pl.ds(start, L_M), :]
        va = v_ref[pl.ds(start, L_M), :]
        oa = o_ref[pl.ds(start, L_M), :].astype(F32)
        g = gate_ref[pl.ds(start, L_M), :]
        lf = jnp.minimum(g, 0.0) - jnp.log1p(jnp.exp(-jnp.abs(g)))
        hi, mid, lo = _split3(lf)
        b_all = (jnp.dot(tri_b, hi, preferred_element_type=F32)
                 + jnp.dot(tri_b, mid, preferred_element_type=F32)
                 + jnp.dot(tri_b, lo, preferred_element_type=F32))
        zt = jnp.where(lane < H_M, g, b_all).T
        m_out = []
        for hh in range(H_M):
            sl = slice(hh * DH_M, (hh + 1) * DH_M)
            i_rep = jnp.broadcast_to(g[:, hh:hh + 1], (L_M, LANES))
            b_rep = jnp.broadcast_to(b_all[:, H_M + hh:H_M + hh + 1], (L_M, LANES))
            r_row = zt[hh:hh + 1, :] - zt[H_M + hh:H_M + hh + 1, :]
            m_p = m_prev[hh]
            dmat = jnp.where(tri, _twice(b_rep) + r_row, -jnp.inf)
            inter = b_rep + m_p
            m_t = jnp.maximum(inter, jnp.max(dmat, axis=-1, keepdims=True))
            qh = qa[:, sl]
            kh = ka[:, sl]
            s_qk = lax.dot_general(qh, kh, (((1,), (1,)), ((), ())), preferred_element_type=F32)
            w_intra = (jnp.exp(dmat - _twice(m_t)) * s_qk).astype(BF16)
            w_inter = jnp.exp(inter - m_t)
            v_aug = jnp.concatenate([va[:, sl], ones_blk], axis=1)
            s_old = s_ref[hh]
            tot = (_twice(w_inter) * jnp.dot(qh, s_old.astype(BF16), preferred_element_type=F32)
                   + jnp.dot(w_intra, v_aug, preferred_element_type=F32))
            num = tot[:, :DH_M]
            den = tot[:, DH_M:]
            hval = num / jnp.maximum(jnp.abs(den), jnp.exp(-m_t))
            b_last = b_rep[L_M - 1:L_M, :]
            log_s = b_last - b_rep + i_rep
            m_new = jnp.maximum(b_last + m_p, jnp.max(log_s, axis=0, keepdims=True))
            w_s = jnp.exp(log_s - m_new)
            decay = jnp.exp(b_last + m_p - m_new)
            kw = (kh.astype(F32) * w_s).astype(BF16)
            s_ref[hh] = _twice(decay) * s_old + lax.dot_general(
                kw, v_aug, (((0,), (0,)), ((), ())), preferred_element_type=F32)
            m_out.append(m_new)
            hn = hval * lax.rsqrt(jnp.mean(hval * hval, axis=-1, keepdims=True) + EPS)
            out_ref[pl.ds(start, L_M), sl] = (
                hn * g_ref[:, sl] * _sigmoid(oa[:, sl])).astype(BF16)
        return tuple(m_out)

    m0 = tuple(jnp.zeros((1, LANES), F32) for _ in range(H_M))
    lax.fori_loop(0, n_chunks, body, m0)


def _mlstm(mm, gates, g_mlstm, bsz, seq):
    t = mm.shape[0]
    return pl.pallas_call(
        _mlstm_kernel,
        grid=(bsz,),
        in_specs=[
            pl.BlockSpec((seq, W_M), lambda b: (b, 0)),
            pl.BlockSpec((seq, W_M), lambda b: (b, 1)),
            pl.BlockSpec((seq, W_M), lambda b: (b, 2)),
            pl.BlockSpec((seq, W_M), lambda b: (b, 3)),
            pl.BlockSpec((seq, GATE_COLS), lambda b: (b, 0)),
            pl.BlockSpec((1, W_M), lambda b: (0, 0)),
        ],
        out_specs=pl.BlockSpec((seq, W_M), lambda b: (b, 0)),
        out_shape=jax.ShapeDtypeStruct((t, W_M), BF16),
        scratch_shapes=[pltpu.VMEM((H_M, DH_M, 2 * DH_M), F32)],
        compiler_params=pltpu.CompilerParams(
            dimension_semantics=("arbitrary",), vmem_limit_bytes=VMEM_LIMIT),
        name="mlstm",
    )(mm, mm, mm, mm, gates, g_mlstm)


def _diffattn_kernel(q_ref, k_ref, v_ref, lq1_ref, lk1_ref, lq2_ref, lk2_ref, g_ref,
                     out_ref, vaug_ref, *, lam_init):
    seq = q_ref.shape[0]
    lane = lax.broadcasted_iota(jnp.int32, (TQ, LANES), 1)
    comp0 = lane < DQK_D
    vaug_ref[:, :DV_D] = v_ref[...]
    vaug_ref[:, DV_D:] = jnp.ones((seq, DV_D), BF16)
    row = lax.broadcasted_iota(jnp.int32, (2 * TQ, TQ), 0)
    col = lax.broadcasted_iota(jnp.int32, (2 * TQ, TQ), 1)
    causal = col <= (row & (TQ - 1))
    lam = (jnp.exp(jnp.sum(lq1_ref[...] * lk1_ref[...], axis=-1, keepdims=True))
           - jnp.exp(jnp.sum(lq2_ref[...] * lk2_ref[...], axis=-1, keepdims=True)) + lam_init)
    for qi in reversed(range(seq // TQ)):
        n_keys = (qi + 1) * TQ
        q = q_ref[qi * TQ:(qi + 1) * TQ, :]
        zero = jnp.zeros_like(q)
        q2 = jnp.concatenate([jnp.where(comp0, q, zero), jnp.where(comp0, zero, q)], axis=0)
        s_diag = lax.dot_general(q2, k_ref[n_keys - TQ:n_keys, :], (((1,), (1,)), ((), ())),
                                 preferred_element_type=F32)
        s_diag = jnp.where(causal, s_diag, -jnp.inf)
        m = jnp.max(s_diag, axis=-1, keepdims=True)
        if qi > 0:
            s_past = lax.dot_general(q2, k_ref[:n_keys - TQ, :], (((1,), (1,)), ((), ())),
                                     preferred_element_type=F32)
            m = jnp.maximum(m, jnp.max(s_past, axis=-1, keepdims=True))
            p = jnp.concatenate([jnp.exp((s_past - m).astype(BF16)),
                                 jnp.exp((s_diag - m).astype(BF16))], axis=1)
        else:
            p = jnp.exp((s_diag - m).astype(BF16))
        a = jnp.dot(p, vaug_ref[:n_keys, :], preferred_element_type=F32)
        a0 = a[:TQ]
        a1 = a[TQ:]
        o = a0[:, :DV_D] / a0[:, DV_D:] - lam * (a1[:, :DV_D] / a1[:, DV_D:])
        on = o * lax.rsqrt(jnp.mean(o * o, axis=-1, keepdims=True) + EPS)
        out_ref[qi * TQ:(qi + 1) * TQ, :] = (on * g_ref[...] * (1.0 - lam_init)).astype(BF16)


def _diffattn(dd, lq1, lk1, lq2, lk2, g_diff, bsz, seq, lam_init):
    t = dd.shape[0]
    lam_spec = pl.BlockSpec((1, DQK_D), lambda b, h: (0, 0))
    return pl.pallas_call(
        functools.partial(_diffattn_kernel, lam_init=lam_init),
        grid=(bsz, H_D),
        in_specs=[
            pl.BlockSpec((seq, DV_D), lambda b, h: (b, h)),
            pl.BlockSpec((seq, DV_D), lambda b, h: (b, H_D + h)),
            pl.BlockSpec((seq, DV_D), lambda b, h: (b, 2 * H_D + h)),
            lam_spec, lam_spec, lam_spec, lam_spec,
            pl.BlockSpec((1, DV_D), lambda b, h: (0, h)),
        ],
        out_specs=pl.BlockSpec((seq, DV_D), lambda b, h: (b, h)),
        out_shape=jax.ShapeDtypeStruct((t, W_D), BF16),
        scratch_shapes=[pltpu.VMEM((seq, 2 * DV_D), BF16)],
        compiler_params=pltpu.CompilerParams(
            dimension_semantics=("arbitrary", "arbitrary"), vmem_limit_bytes=VMEM_LIMIT),
        name="diffattn",
    )(dd, dd, dd, lq1, lk1, lq2, lk2, g_diff)


def _ffn_kernel(x_ref, hm_ref, hd_ref, gta_ref, shf_ref, scf_ref, gtf_ref, gffn_ref, gfin_ref,
                wout_ref, wup_ref, cw_ref, cb_ref, wdown_ref, out_ref, carry_ref,
                *, tiles_per_seq):
    i = pl.program_id(0)
    seq_start = (i % tiles_per_seq) == 0
    mix = (jnp.dot(hm_ref[...], wout_ref[:W_M, :], preferred_element_type=F32)
           + jnp.dot(hd_ref[...], wout_ref[W_M:, :], preferred_element_type=F32))
    x1 = x_ref[...] + gta_ref[...] * mix
    y = x1 * lax.rsqrt(jnp.mean(x1 * x1, axis=-1, keepdims=True) + EPS) * gffn_ref[...]
    h2 = (y * (1.0 + scf_ref[...]) + shf_ref[...]).astype(BF16)
    tm = h2.shape[0]

    def conv_cols(col0, width):
        cols = slice(col0, col0 + width)
        p = jnp.dot(h2, wup_ref[:, cols], preferred_element_type=F32)
        prev = jnp.where(seq_start, 0.0, carry_ref[:, cols])
        carry_ref[:, cols] = p[tm - SUBLANES:, :]
        u = p * cw_ref[CONV_FFN - 1:CONV_FFN, cols] + cb_ref[:, cols]
        for sh, shifted in enumerate(_shifted_rows(p, prev, CONV_FFN - 1), start=1):
            u = u + shifted * cw_ref[CONV_FFN - 1 - sh:CONV_FFN - sh, cols]
        return u

    acc = jnp.zeros((tm, D_MODEL), F32)
    for c0, width in FF_CHUNKS:
        a = conv_cols(c0, width)
        g = conv_cols(D_FF + c0, width)
        act = (g * _sigmoid(g) * a).astype(BF16)
        acc = acc + jnp.dot(act, wdown_ref[c0:c0 + width, :], preferred_element_type=F32)
    x2 = x1 + gtf_ref[...] * acc
    out_ref[...] = x2 * lax.rsqrt(jnp.mean(x2 * x2, axis=-1, keepdims=True) + EPS) * gfin_ref[...]


def _ffn(x3, hm, hd, mod3, g_ffn, g_final, w_out_b, w_up_b, conv_w, conv_b, w_down_b):
    bsz, seq, d = x3.shape
    t = bsz * seq
    tiles_per_seq = seq // TM_FFN

    def mod_spec(j):
        return pl.BlockSpec((None, 1, d), lambda i: (i // tiles_per_seq, 0, j))

    def const(shape):
        return pl.BlockSpec(shape, lambda i: (0,) * len(shape), pipeline_mode=pl.Buffered(1))

    x_spec = pl.BlockSpec((None, TM_FFN, d), lambda i: (i // tiles_per_seq, i % tiles_per_seq, 0))
    return pl.pallas_call(
        functools.partial(_ffn_kernel, tiles_per_seq=tiles_per_seq),
        grid=(t // TM_FFN,),
        in_specs=[
            x_spec,
            pl.BlockSpec((TM_FFN, W_M), lambda i: (i, 0)),
            pl.BlockSpec((TM_FFN, W_D), lambda i: (i, 0)),
            mod_spec(2), mod_spec(3), mod_spec(4), mod_spec(5),
            const((1, d)), const((1, d)),
            const((d, d)),
            const((d, 2 * D_FF)),
            const((CONV_FFN, 2 * D_FF)),
            const((1, 2 * D_FF)),
            const((D_FF, d)),
        ],
        out_specs=x_spec,
        out_shape=jax.ShapeDtypeStruct((bsz, seq, d), F32),
        scratch_shapes=[pltpu.VMEM((SUBLANES, 2 * D_FF), F32)],
        compiler_params=pltpu.CompilerParams(
            dimension_semantics=("arbitrary",), vmem_limit_bytes=VMEM_LIMIT),
        name="ffn",
    )(x3, hm, hd, mod3, mod3, mod3, mod3, g_ffn, g_final, w_out_b, w_up_b, conv_w, conv_b, w_down_b)


def kernel(x, c, positions, w_ada, b_ada, g_mix, w_in, conv_qk_w, conv_qk_b, b_if, g_mlstm,
           lam_q1, lam_k1, lam_q2, lam_k2, g_diff, w_out, g_ffn, w_up, conv_ffn_w, conv_ffn_b,
           w_down, g_final):
    bsz, seq, d = x.shape
    assert w_ada.shape[0] == DEPTH == 1
    l = 0
    n_gate = 2 * H_M
    lam_init = 0.8 - 0.6 * math.exp(-0.3 * l)

    half = DQK_D // 2
    inv_freq = ROPE_THETA ** (-jnp.arange(half, dtype=F32) / half)
    ang = inv_freq[:, None] * positions.astype(F32).reshape(1, bsz * seq)
    rope_t = jnp.concatenate([jnp.cos(ang), jnp.sin(ang), jnp.zeros((LANES - DQK_D, bsz * seq), F32)], axis=0)

    w_t = jnp.transpose(w_in[l])
    w_mm = w_t[:MM_COLS].astype(BF16)
    q_scale = jnp.concatenate([jnp.full((W_D,), DQK_D ** -0.5, F32), jnp.ones((2 * W_D,), F32)])
    w_dd = (w_t[MM_COLS + n_gate:] * q_scale[:, None]).astype(BF16)
    w_g = jnp.pad(w_t[MM_COLS:MM_COLS + n_gate], ((0, GATE_COLS - n_gate), (0, 0))).astype(BF16)
    bif_p = jnp.pad(b_if[l], (0, GATE_COLS - n_gate)).reshape(1, GATE_COLS)

    mod3 = _adaln(c, w_ada[l], b_ada[l]).reshape(bsz, 1, 6 * d)
    mm, dd, gates = _inproj(x, mod3, g_mix[l].reshape(1, d), w_mm, w_dd, w_g, rope_t, bif_p,
                            conv_qk_w[l], conv_qk_b[l].reshape(1, -1))
    hm = _mlstm(mm, gates, g_mlstm[l].reshape(1, -1), bsz, seq)
    hd = _diffattn(dd, lam_q1[l].reshape(1, -1), lam_k1[l].reshape(1, -1),
                   lam_q2[l].reshape(1, -1), lam_k2[l].reshape(1, -1),
                   g_diff[l].reshape(1, -1), bsz, seq, lam_init)
    return _ffn(x, hm, hd, mod3, g_ffn[l].reshape(1, d), g_final.reshape(1, d),
                w_out[l].astype(BF16), w_up[l].astype(BF16), conv_ffn_w[l],
                conv_ffn_b[l].reshape(1, -1), w_down[l].astype(BF16))
```

```python
import functools
import math

import jax
import jax.numpy as jnp
from jax import lax
from jax.experimental import pallas as pl
from jax.experimental.pallas import tpu as pltpu

F32 = jnp.float32
BF16 = jnp.bfloat16

D_MODEL = 1024
DEPTH = 1
W_M = D_MODEL // 2
H_M = 4
DH_M = W_M // H_M
CONV_QK = 4
W_D = D_MODEL - W_M
H_D = 4
DV_D = W_D // H_D
DQK_D = DV_D // 2
D_FF = ((8 * D_MODEL) // 3 + 127) // 128 * 128
CONV_FFN = 3
ROPE_THETA = 10000.0
EPS = 1e-6

LANES = 128
SUBLANES = 8
GATE_COLS = LANES
MM_COLS = 4 * W_M
DD_COLS = 3 * W_D

TM_IN = 1024
L_M = 256
TQ = 256
ATT_HEADS = 2
TM_FFN = 512
MXU_DIM = 256
FF_CHUNKS = ((0, 6 * MXU_DIM), (6 * MXU_DIM, D_FF - 6 * MXU_DIM))
VMEM_LIMIT = 56 * 1024 * 1024
NT_DIMS = (((1,), (1,)), ((), ()))


def _sigmoid(v):
    return 1.0 / (1.0 + jnp.exp(-v))


def _split3(v):
    hi = v.astype(BF16)
    r1 = v - hi.astype(F32)
    mid = r1.astype(BF16)
    lo = (r1 - mid.astype(F32)).astype(BF16)
    return hi, mid, lo


def _twice(v):
    return jnp.concatenate([v, v], axis=1)


def _adaln_kernel(c_ref, w_ref, b_ref, o_ref):
    c = c_ref[...]
    ca = (c * _sigmoid(c)).astype(BF16)
    o_ref[...] = jnp.dot(ca, w_ref[...].astype(BF16), preferred_element_type=F32) + b_ref[...]


def _adaln(c, w_ada, b_ada):
    bsz, d = c.shape
    n = w_ada.shape[1]
    tn = 1024
    return pl.pallas_call(
        _adaln_kernel,
        grid=(n // tn,),
        in_specs=[
            pl.BlockSpec((bsz, d), lambda j: (0, 0)),
            pl.BlockSpec((d, tn), lambda j: (0, j)),
            pl.BlockSpec((1, tn), lambda j: (0, j)),
        ],
        out_specs=pl.BlockSpec((bsz, tn), lambda j: (0, j)),
        out_shape=jax.ShapeDtypeStruct((bsz, n), F32),
        compiler_params=pltpu.CompilerParams(
            dimension_semantics=("arbitrary",), vmem_limit_bytes=VMEM_LIMIT),
        name="adaln",
    )(c, w_ada, b_ada.reshape(1, n))


def _shifted_rows(cur, prev, n_shift):
    ext = jnp.concatenate([prev, cur], axis=0)
    return [pltpu.roll(ext, sh, 0)[SUBLANES:, :] for sh in range(1, n_shift + 1)]


def _inproj_kernel(x_ref, sh_ref, sc_ref, g_ref, wmm_ref, wdd_ref, wg_ref, rope_ref, bif_ref,
                   cw_ref, cb_ref, mm_ref, dd_ref, gate_ref, carry_ref, *, tiles_per_seq):
    seq_start = (pl.program_id(0) % tiles_per_seq) == 0
    x = x_ref[...]
    tm = x.shape[0]
    ms = jnp.mean(x * x, axis=-1, keepdims=True)
    y = x * lax.rsqrt(ms + EPS) * g_ref[...]
    h = (y * (1.0 + sc_ref[...]) + sh_ref[...]).astype(BF16)
    pm = lax.dot_general(h, wmm_ref[...], NT_DIMS, preferred_element_type=F32)
    mm_ref[:, 2 * W_M:] = pm[:, 2 * W_M:].astype(BF16)
    qk = pm[:, :2 * W_M]
    prev = jnp.where(seq_start, 0.0, carry_ref[...])
    carry_ref[...] = qk[tm - SUBLANES:, :]
    u = qk * cw_ref[CONV_QK - 1:CONV_QK, :] + cb_ref[...]
    for sh, shifted in enumerate(_shifted_rows(qk, prev, CONV_QK - 1), start=1):
        u = u + shifted * cw_ref[CONV_QK - 1 - sh:CONV_QK - sh, :]
    u = u * _sigmoid(u)
    mm_ref[:, :W_M] = u[:, :W_M].astype(BF16)
    mm_ref[:, W_M:2 * W_M] = (u[:, W_M:] * (DH_M ** -0.5)).astype(BF16)
    gate_ref[...] = lax.dot_general(h, wg_ref[...], NT_DIMS, preferred_element_type=F32) + bif_ref[...]
    pd = lax.dot_general(h, wdd_ref[...], NT_DIMS, preferred_element_type=F32)
    cs = rope_ref[...].T
    c32 = cs[:, :DQK_D // 2]
    s32 = cs[:, DQK_D // 2:DQK_D]
    cos = jnp.concatenate([c32, c32, c32, c32], axis=1)
    sin = jnp.concatenate([-s32, s32, -s32, s32], axis=1)
    lane = lax.broadcasted_iota(jnp.int32, cos.shape, 1)
    first_half = (lane & (DQK_D // 2)) == 0
    for j in range(2 * W_D // LANES):
        xs = pd[:, j * LANES:(j + 1) * LANES]
        partner = jnp.where(first_half,
                            pltpu.roll(xs, LANES - DQK_D // 2, 1),
                            pltpu.roll(xs, DQK_D // 2, 1))
        dd_ref[:, j * LANES:(j + 1) * LANES] = (xs * cos + partner * sin).astype(BF16)
    dd_ref[:, 2 * W_D:] = pd[:, 2 * W_D:].astype(BF16)


def _inproj(x3, mod3, g_mix, w_mm, w_dd, w_g, rope_t, bif_p, conv_w, conv_b):
    bsz, seq, d = x3.shape
    t = bsz * seq
    tiles_per_seq = seq // TM_IN

    def seq_spec(width):
        return pl.BlockSpec((None, TM_IN, width), lambda i: (i // tiles_per_seq, i % tiles_per_seq, 0))

    def const(shape):
        return pl.BlockSpec(shape, lambda i: (0,) * len(shape))

    return pl.pallas_call(
        functools.partial(_inproj_kernel, tiles_per_seq=tiles_per_seq),
        grid=(t // TM_IN,),
        in_specs=[
            seq_spec(d),
            pl.BlockSpec((None, 1, d), lambda i: (i // tiles_per_seq, 0, 0)),
            pl.BlockSpec((None, 1, d), lambda i: (i // tiles_per_seq, 0, 1)),
            const((1, d)),
            const((MM_COLS, d)), const((DD_COLS, d)), const((GATE_COLS, d)),
            pl.BlockSpec((LANES, TM_IN), lambda i: (0, i)),
            const((1, GATE_COLS)),
            const((CONV_QK, 2 * W_M)), const((1, 2 * W_M)),
        ],
        out_specs=[
            pl.BlockSpec((TM_IN, MM_COLS), lambda i: (i, 0)),
            pl.BlockSpec((TM_IN, DD_COLS), lambda i: (i, 0)),
            pl.BlockSpec((TM_IN, GATE_COLS), lambda i: (i, 0)),
        ],
        out_shape=[
            jax.ShapeDtypeStruct((t, MM_COLS), BF16),
            jax.ShapeDtypeStruct((t, DD_COLS), BF16),
            jax.ShapeDtypeStruct((t, GATE_COLS), F32),
        ],
        scratch_shapes=[pltpu.VMEM((SUBLANES, 2 * W_M), F32)],
        compiler_params=pltpu.CompilerParams(
            dimension_semantics=("arbitrary",), vmem_limit_bytes=VMEM_LIMIT),
        name="inproj",
    )(x3, mod3, mod3, g_mix, w_mm, w_dd, w_g, rope_t, bif_p, conv_w, conv_b)


def _mlstm_kernel(q_ref, k_ref, v_ref, o_ref, gate_ref, g_ref, out_ref, s_ref):
    seq = q_ref.shape[0]
    n_chunks = seq // L_M
    s_ref[...] = jnp.zeros_like(s_ref)
    row = lax.broadcasted_iota(jnp.int32, (L_M, L_M), 0)
    col = lax.broadcasted_iota(jnp.int32, (L_M, L_M), 1)
    tri = row >= col
    tri_b = tri.astype(BF16)
    lane = lax.broadcasted_iota(jnp.int32, (L_M, LANES), 1)
    ones_blk = jnp.ones((L_M, LANES), BF16)

    def body(c, m_prev):
        start = pl.multiple_of(c * L_M, L_M)
        qa = q_ref[pl.ds(start, L_M), :]
        ka = k_ref[pl.ds(start, L_M), :]
        va = v_ref[pl.ds(start, L_M), :]
        oa = o_ref[pl.ds(start, L_M), :].astype(F32)
        g = gate_ref[pl.ds(start, L_M), :]
        lf = jnp.minimum(g, 0.0) - jnp.log(1.0 + jnp.exp(-jnp.abs(g)))
        hi, mid, lo = _split3(lf)
        b_all = (jnp.dot(tri_b, hi, preferred_element_type=F32)
                 + jnp.dot(tri_b, mid, preferred_element_type=F32)
                 + jnp.dot(tri_b, lo, preferred_element_type=F32))
        zt = jnp.where(lane < H_M, g, b_all).T
        m_out = []
        for hh in range(H_M):
            sl = slice(hh * DH_M, (hh + 1) * DH_M)
            i_rep = jnp.broadcast_to(g[:, hh:hh + 1], (L_M, LANES))
            b_rep = jnp.broadcast_to(b_all[:, H_M + hh:H_M + hh + 1], (L_M, LANES))
            r_row = zt[hh:hh + 1, :] - zt[H_M + hh:H_M + hh + 1, :]
            m_p = m_prev[hh]
            dmat = jnp.where(tri, _twice(b_rep) + r_row, -jnp.inf)
            inter = b_rep + m_p
            m_t = jnp.maximum(inter, jnp.max(dmat, axis=-1, keepdims=True))
            qh = qa[:, sl]
            kh = ka[:, sl]
            s_qk = lax.dot_general(qh, kh, (((1,), (1,)), ((), ())), preferred_element_type=F32)
            w_intra = (jnp.exp(dmat - _twice(m_t)) * s_qk).astype(BF16)
            w_inter = jnp.exp(inter - m_t)
            v_aug = jnp.concatenate([va[:, sl], ones_blk], axis=1)
            s_old = s_ref[hh]
            tot = (_twice(w_inter) * jnp.dot(qh, s_old.astype(BF16), preferred_element_type=F32)
                   + jnp.dot(w_intra, v_aug, preferred_element_type=F32))
            num = tot[:, :DH_M]
            den = tot[:, DH_M:]
            hval = num / jnp.maximum(jnp.abs(den), jnp.exp(-m_t))
            b_last = b_rep[L_M - 1:L_M, :]
            log_s = b_last - b_rep + i_rep
            m_new = jnp.maximum(b_last + m_p, jnp.max(log_s, axis=0, keepdims=True))
            w_s = jnp.exp(log_s - m_new)
            decay = jnp.exp(b_last + m_p - m_new)
            kw = (kh.astype(F32) * w_s).astype(BF16)
            s_ref[hh] = _twice(decay) * s_old + lax.dot_general(
                kw, v_aug, (((0,), (0,)), ((), ())), preferred_element_type=F32)
            m_out.append(m_new)
            hn = hval * lax.rsqrt(jnp.mean(hval * hval, axis=-1, keepdims=True) + EPS)
            out_ref[pl.ds(start, L_M), sl] = (
                hn * g_ref[:, sl] * _sigmoid(oa[:, sl])).astype(BF16)
        return tuple(m_out)

    m0 = tuple(jnp.zeros((1, LANES), F32) for _ in range(H_M))
    lax.fori_loop(0, n_chunks, body, m0)


def _mlstm(mm, gates, g_mlstm, bsz, seq):
    t = mm.shape[0]
    return pl.pallas_call(
        _mlstm_kernel,
        grid=(bsz,),
        in_specs=[
            pl.BlockSpec((seq, W_M), lambda b: (b, 0)),
            pl.BlockSpec((seq, W_M), lambda b: (b, 1)),
            pl.BlockSpec((seq, W_M), lambda b: (b, 2)),
            pl.BlockSpec((seq, W_M), lambda b: (b, 3)),
            pl.BlockSpec((seq, GATE_COLS), lambda b: (b, 0)),
            pl.BlockSpec((1, W_M), lambda b: (0, 0)),
        ],
        out_specs=pl.BlockSpec((seq, W_M), lambda b: (b, 0)),
        out_shape=jax.ShapeDtypeStruct((t, W_M), BF16),
        scratch_shapes=[pltpu.VMEM((H_M, DH_M, 2 * DH_M), F32)],
        compiler_params=pltpu.CompilerParams(
            dimension_semantics=("arbitrary",), vmem_limit_bytes=VMEM_LIMIT),
        name="mlstm",
    )(mm, mm, mm, mm, gates, g_mlstm)


def _diffattn_kernel(q_ref, k_ref, v_ref, lq1_ref, lk1_ref, lq2_ref, lk2_ref, g_ref,
                     out_ref, vaug_ref, *, lam_init):
    seq = q_ref.shape[0]
    lane = lax.broadcasted_iota(jnp.int32, (TQ, LANES), 1)
    comp0 = lane < DQK_D
    for hh in range(ATT_HEADS):
        vaug_ref[hh, :, :DV_D] = v_ref[:, hh * DV_D:(hh + 1) * DV_D]
        vaug_ref[hh, :, DV_D:] = jnp.ones((seq, DV_D), BF16)
    row = lax.broadcasted_iota(jnp.int32, (2 * TQ, TQ), 0)
    col = lax.broadcasted_iota(jnp.int32, (2 * TQ, TQ), 1)
    causal = col <= (row & (TQ - 1))
    lam = (jnp.exp(jnp.sum(lq1_ref[...] * lk1_ref[...], axis=-1, keepdims=True))
           - jnp.exp(jnp.sum(lq2_ref[...] * lk2_ref[...], axis=-1, keepdims=True)) + lam_init)

    def block(qi, hh):
        hs = slice(hh * DV_D, (hh + 1) * DV_D)
        n_keys = (qi + 1) * TQ
        q = q_ref[qi * TQ:(qi + 1) * TQ, hs]
        zero = jnp.zeros_like(q)
        q2 = jnp.concatenate([jnp.where(comp0, q, zero), jnp.where(comp0, zero, q)], axis=0)
        s_diag = lax.dot_general(q2, k_ref[n_keys - TQ:n_keys, hs], NT_DIMS,
                                 preferred_element_type=F32)
        s_diag = jnp.where(causal, s_diag, -jnp.inf)
        m = jnp.max(s_diag, axis=-1, keepdims=True)
        if qi > 0:
            s_past = lax.dot_general(q2, k_ref[:n_keys - TQ, hs], NT_DIMS,
                                     preferred_element_type=F32)
            m = jnp.maximum(m, jnp.max(s_past, axis=-1, keepdims=True))
            p = jnp.concatenate([jnp.exp((s_past - m).astype(BF16)),
                                 jnp.exp((s_diag - m).astype(BF16))], axis=1)
        else:
            p = jnp.exp((s_diag - m).astype(BF16))
        a = jnp.dot(p, vaug_ref[hh, :n_keys, :], preferred_element_type=F32)
        a0 = a[:TQ]
        a1 = a[TQ:]
        o = a0[:, :DV_D] / a0[:, DV_D:] - lam * (a1[:, :DV_D] / a1[:, DV_D:])
        on = o * lax.rsqrt(jnp.mean(o * o, axis=-1, keepdims=True) + EPS)
        out_ref[qi * TQ:(qi + 1) * TQ, hs] = (on * g_ref[:, hs] * (1.0 - lam_init)).astype(BF16)

    for qi in reversed(range(seq // TQ)):
        for hh in range(ATT_HEADS):
            block(qi, hh)


def _diffattn(dd, lq1, lk1, lq2, lk2, g_diff, bsz, seq, lam_init):
    t = dd.shape[0]
    wid = ATT_HEADS * DV_D
    groups = H_D // ATT_HEADS
    lam_spec = pl.BlockSpec((1, DQK_D), lambda b, h: (0, 0))
    return pl.pallas_call(
        functools.partial(_diffattn_kernel, lam_init=lam_init),
        grid=(bsz, groups),
        in_specs=[
            pl.BlockSpec((seq, wid), lambda b, h: (b, h)),
            pl.BlockSpec((seq, wid), lambda b, h: (b, groups + h)),
            pl.BlockSpec((seq, wid), lambda b, h: (b, 2 * groups + h)),
            lam_spec, lam_spec, lam_spec, lam_spec,
            pl.BlockSpec((1, wid), lambda b, h: (0, h)),
        ],
        out_specs=pl.BlockSpec((seq, wid), lambda b, h: (b, h)),
        out_shape=jax.ShapeDtypeStruct((t, W_D), BF16),
        scratch_shapes=[pltpu.VMEM((ATT_HEADS, seq, 2 * DV_D), BF16)],
        compiler_params=pltpu.CompilerParams(
            dimension_semantics=("arbitrary", "arbitrary"), vmem_limit_bytes=VMEM_LIMIT),
        name="diffattn",
    )(dd, dd, dd, lq1, lk1, lq2, lk2, g_diff)


def _ffn_kernel(x_ref, hm_ref, hd_ref, gta_ref, shf_ref, scf_ref, gtf_ref, gffn_ref, gfin_ref,
                wout_ref, wup_ref, cw_ref, cb_ref, wdown_ref, out_ref, carry_ref,
                *, tiles_per_seq):
    i = pl.program_id(0)
    seq_start = (i % tiles_per_seq) == 0
    mix = (jnp.dot(hm_ref[...], wout_ref[:W_M, :], preferred_element_type=F32)
           + jnp.dot(hd_ref[...], wout_ref[W_M:, :], preferred_element_type=F32))
    x1 = x_ref[...] + gta_ref[...] * mix
    y = x1 * lax.rsqrt(jnp.mean(x1 * x1, axis=-1, keepdims=True) + EPS) * gffn_ref[...]
    h2 = (y * (1.0 + scf_ref[...]) + shf_ref[...]).astype(BF16)
    tm = h2.shape[0]

    def conv_cols(col0, width):
        cols = slice(col0, col0 + width)
        p = jnp.dot(h2, wup_ref[:, cols], preferred_element_type=F32)
        prev = jnp.where(seq_start, 0.0, carry_ref[:, cols])
        carry_ref[:, cols] = p[tm - SUBLANES:, :]
        u = p * cw_ref[CONV_FFN - 1:CONV_FFN, cols] + cb_ref[:, cols]
        for sh, shifted in enumerate(_shifted_rows(p, prev, CONV_FFN - 1), start=1):
            u = u + shifted * cw_ref[CONV_FFN - 1 - sh:CONV_FFN - sh, cols]
        return u

    acc = jnp.zeros((tm, D_MODEL), F32)
    for c0, width in FF_CHUNKS:
        a = conv_cols(c0, width)
        g = conv_cols(D_FF + c0, width)
        act = (g * _sigmoid(g) * a).astype(BF16)
        acc = acc + jnp.dot(act, wdown_ref[c0:c0 + width, :], preferred_element_type=F32)
    x2 = x1 + gtf_ref[...] * acc
    out_ref[...] = x2 * lax.rsqrt(jnp.mean(x2 * x2, axis=-1, keepdims=True) + EPS) * gfin_ref[...]


def _ffn(x3, hm, hd, mod3, g_ffn, g_final, w_out_b, w_up_b, conv_w, conv_b, w_down_b):
    bsz, seq, d = x3.shape
    t = bsz * seq
    tiles_per_seq = seq // TM_FFN

    def mod_spec(j):
        return pl.BlockSpec((None, 1, d), lambda i: (i // tiles_per_seq, 0, j))

    def const(shape):
        return pl.BlockSpec(shape, lambda i: (0,) * len(shape), pipeline_mode=pl.Buffered(1))

    x_spec = pl.BlockSpec((None, TM_FFN, d), lambda i: (i // tiles_per_seq, i % tiles_per_seq, 0))
    return pl.pallas_call(
        functools.partial(_ffn_kernel, tiles_per_seq=tiles_per_seq),
        grid=(t // TM_FFN,),
        in_specs=[
            x_spec,
            pl.BlockSpec((TM_FFN, W_M), lambda i: (i, 0)),
            pl.BlockSpec((TM_FFN, W_D), lambda i: (i, 0)),
            mod_spec(2), mod_spec(3), mod_spec(4), mod_spec(5),
            const((1, d)), const((1, d)),
            const((d, d)),
            const((d, 2 * D_FF)),
            const((CONV_FFN, 2 * D_FF)),
            const((1, 2 * D_FF)),
            const((D_FF, d)),
        ],
        out_specs=x_spec,
        out_shape=jax.ShapeDtypeStruct((bsz, seq, d), F32),
        scratch_shapes=[pltpu.VMEM((SUBLANES, 2 * D_FF), F32)],
        compiler_params=pltpu.CompilerParams(
            dimension_semantics=("arbitrary",), vmem_limit_bytes=VMEM_LIMIT),
        name="ffn",
    )(x3, hm, hd, mod3, mod3, mod3, mod3, g_ffn, g_final, w_out_b, w_up_b, conv_w, conv_b, w_down_b)


def kernel(x, c, positions, w_ada, b_ada, g_mix, w_in, conv_qk_w, conv_qk_b, b_if, g_mlstm,
           lam_q1, lam_k1, lam_q2, lam_k2, g_diff, w_out, g_ffn, w_up, conv_ffn_w, conv_ffn_b,
           w_down, g_final):
    bsz, seq, d = x.shape
    assert w_ada.shape[0] == DEPTH == 1
    l = 0
    n_gate = 2 * H_M
    lam_init = 0.8 - 0.6 * math.exp(-0.3 * l)

    half = DQK_D // 2
    inv_freq = ROPE_THETA ** (-jnp.arange(half, dtype=F32) / half)
    ang = inv_freq[:, None] * positions.astype(F32).reshape(1, bsz * seq)
    rope_t = jnp.concatenate([jnp.cos(ang), jnp.sin(ang), jnp.zeros((LANES - DQK_D, bsz * seq), F32)], axis=0)

    w_t = jnp.transpose(w_in[l])
    w_mm = w_t[:MM_COLS].astype(BF16)
    q_scale = jnp.concatenate([jnp.full((W_D,), DQK_D ** -0.5, F32), jnp.ones((2 * W_D,), F32)])
    w_dd = (w_t[MM_COLS + n_gate:] * q_scale[:, None]).astype(BF16)
    w_g = jnp.pad(w_t[MM_COLS:MM_COLS + n_gate], ((0, GATE_COLS - n_gate), (0, 0))).astype(BF16)
    bif_p = jnp.pad(b_if[l], (0, GATE_COLS - n_gate)).reshape(1, GATE_COLS)

    mod3 = _adaln(c, w_ada[l], b_ada[l]).reshape(bsz, 1, 6 * d)
    mm, dd, gates = _inproj(x, mod3, g_mix[l].reshape(1, d), w_mm, w_dd, w_g, rope_t, bif_p,
                            conv_qk_w[l], conv_qk_b[l].reshape(1, -1))
    hm = _mlstm(mm, gates, g_mlstm[l].reshape(1, -1), bsz, seq)
    hd = _diffattn(dd, lam_q1[l].reshape(1, -1), lam_k1[l].reshape(1, -1),
                   lam_q2[l].reshape(1, -1), lam_k2[l].reshape(1, -1),
                   g_diff[l].reshape(1, -1), bsz, seq, lam_init)
    return _ffn(x, hm, hd, mod3, g_ffn[l].reshape(1, d), g_final.reshape(1, d),
                w_out[l].astype(BF16), w_up[l].astype(BF16), conv_ffn_w[l],
                conv_ffn_b[l].reshape(1, -1), w_down[l].astype(BF16))
```
